```python
import math
import jax, jax.numpy as jnp
from jax import lax
import numpy as np

D_MODEL = 2048
BATCH = 2
SEQ = 4096
DEPTH = 4
DEC_BATCH = 8
DEC_SEQ = 1
PAST_LEN = 16384
PAGE_SIZE = 128

W_BRANCH = D_MODEL // 2
N_BRANCH = 3
GMLP_GROUPS = 4
GMLP_CHUNK = 128
SSM_HEAD_DIM = 64
SSM_HEADS = W_BRANCH // SSM_HEAD_DIM
SSM_GROUPS = 2
SSM_STATE = 128
SSM_CONV = 4
SSM_CHUNK = 256
CONV_DIM = W_BRANCH + 2 * SSM_GROUPS * SSM_STATE
ATT_HEAD_DIM = 64
ATT_HEADS = W_BRANCH // ATT_HEAD_DIM
MOBA_BLOCK = 256
MOBA_TOPK = 3
Q_BLOCK = 64
EPS = 1e-6
IN_SIZES = [W_BRANCH, W_BRANCH, W_BRANCH,
            CONV_DIM, W_BRANCH, SSM_HEADS,
            W_BRANCH, W_BRANCH, W_BRANCH, W_BRANCH,
            N_BRANCH * D_MODEL]
IN_COLS = sum(IN_SIZES)

kernel_name = 'hybrid_gmlp_ssd_moba_decoder_step'


def _split_points():
    return [int(i) for i in np.cumsum(IN_SIZES)[:-1]]


def rmsnorm(x, g):
    xf = x.astype(jnp.float32)
    y = xf * lax.rsqrt(jnp.mean(xf * xf, axis=-1, keepdims=True) + EPS)
    return (y * g.astype(jnp.float32)).astype(x.dtype)


def layernorm(x, g, b):
    xf = x.astype(jnp.float32)
    mu = jnp.mean(xf, axis=-1, keepdims=True)
    var = jnp.mean(jnp.square(xf - mu), axis=-1, keepdims=True)
    y = (xf - mu) * lax.rsqrt(var + EPS)
    return (y * g.astype(jnp.float32) + b.astype(jnp.float32)).astype(x.dtype)


def alibi_slopes(n):
    return 2.0 ** (-8.0 * jnp.arange(1, n + 1, dtype=jnp.float32) / n)


def chunk_spatial_gate(u, v, w_s, b_s):
    bsz, t, w = u.shape
    n = -(-t // GMLP_CHUNK)
    pad = n * GMLP_CHUNK - t
    vc = jnp.pad(v, ((0, 0), (0, pad), (0, 0))).reshape(bsz, n, GMLP_CHUNK, GMLP_GROUPS, w // GMLP_GROUPS)
    w_causal = w_s * jnp.tril(jnp.ones((GMLP_CHUNK, GMLP_CHUNK), w_s.dtype))
    mixed = jnp.einsum('gts,bcsgd->bctgd', w_causal, vc) + b_s.T[:, :, None]
    return u * mixed.reshape(bsz, n * GMLP_CHUNK, w)[:, :t]


def causal_conv(xbc, prev, w, bias):
    t = xbc.shape[1]
    xcat = jnp.concatenate([prev.astype(xbc.dtype), xbc], axis=1)
    out = bias + xcat[:, 0:t] * w[0]
    for j in range(1, SSM_CONV):
        out = out + xcat[:, j:j + t] * w[j]
    return jax.nn.silu(out), xcat[:, xcat.shape[1] - (SSM_CONV - 1):]


def ssd_scan(x, dt, a, bmat, cmat, h0):
    f32 = jnp.float32
    bsz, t, nh, hp = x.shape
    ng, ns = bmat.shape[2], bmat.shape[3]
    hg = nh // ng
    L = min(SSM_CHUNK, t)
    nc = -(-t // L)
    pad = nc * L - t
    padt = lambda z: jnp.pad(z.astype(f32), [(0, 0), (0, pad)] + [(0, 0)] * (z.ndim - 2))
    xc = padt(x).reshape(bsz, nc, L, ng, hg, hp)
    dtc = padt(dt).reshape(bsz, nc, L, ng, hg)
    bc = padt(bmat).reshape(bsz, nc, L, ng, ns)
    cc = padt(cmat).reshape(bsz, nc, L, ng, ns)
    a_cs = jnp.cumsum(dtc * a.reshape(ng, hg), axis=2)
    causal = jnp.tril(jnp.ones((L, L), bool))[:, :, None, None]
    seg = jnp.exp(jnp.where(causal, a_cs[:, :, :, None] - a_cs[:, :, None, :], -jnp.inf))
    cb = jnp.einsum('bclgn,bcsgn->bclsg', cc, bc)
    w_ls = cb[..., None] * seg * dtc[:, :, None]
    y_diag = jnp.einsum('bclsgh,bcsghp->bclghp', w_ls, xc)
    decay_end = jnp.exp(a_cs[:, :, -1:] - a_cs) * dtc
    states = jnp.einsum('bclgn,bclgh,bclghp->bcghpn', bc, decay_end, xc)
    chunk_decay = jnp.exp(a_cs[:, :, -1])

    def step(h, inp):
        s, d = inp
        return d[..., None, None] * h + s, h

    h_fin, h_prev = lax.scan(step, h0.astype(f32).reshape(bsz, ng, hg, hp, ns),
                             (jnp.moveaxis(states, 1, 0), jnp.moveaxis(chunk_decay, 1, 0)))
    h_prev = jnp.moveaxis(h_prev, 0, 1)
    y_off = jnp.einsum('bclgn,bcghpn,bclgh->bclghp', cc, h_prev, jnp.exp(a_cs))
    y = (y_diag + y_off).reshape(bsz, nc * L, nh, hp)[:, :t]
    return y.astype(x.dtype), h_fin.reshape(bsz, nh, hp, ns)


def moba_attention(q, k, v, slopes):
    f32 = jnp.float32
    bsz, tq, nh, dh = q.shape
    lk = k.shape[1]
    pos0 = lk - tq
    nb = -(-lk // MOBA_BLOCK)
    pad = nb * MOBA_BLOCK - lk
    kb = jnp.pad(k, ((0, 0), (0, pad), (0, 0), (0, 0))).reshape(bsz, nb, MOBA_BLOCK, nh, dh)
    vb = jnp.pad(v, ((0, 0), (0, pad), (0, 0), (0, 0))).reshape(bsz, nb, MOBA_BLOCK, nh, dh)
    k_mean = jnp.mean(kb.astype(f32), axis=2)
    n_sel = min(MOBA_TOPK, nb - 1)
    bi = jnp.arange(bsz)[:, None, None, None]
    hi = jnp.arange(nh)[None, :, None, None]
    offs = jnp.arange(MOBA_BLOCK)
    scale = dh ** -0.5

    def attend(qc, qpos):
        t = qc.shape[1]
        qblk = jnp.minimum(qpos // MOBA_BLOCK, nb - 1)
        own = jnp.broadcast_to(qblk[None, None, :, None], (bsz, nh, t, 1))
        if n_sel > 0:
            gate = jnp.einsum('bqhd,bnhd->bhqn', qc.astype(f32), k_mean)
            eligible = jnp.arange(nb)[None, :] < qblk[:, None]
            gate = jnp.where(eligible, gate, -jnp.inf)
            top_val, top_idx = lax.top_k(gate, n_sel)
            idx = jnp.concatenate([top_idx, own], axis=-1)
            ok = jnp.concatenate([jnp.isfinite(top_val), jnp.ones(own.shape, bool)], axis=-1)
        else:
            idx = own
            ok = jnp.ones(own.shape, bool)
        k_sel = kb[bi, idx, :, hi, :]
        v_sel = vb[bi, idx, :, hi, :]
        kpos = idx[..., None] * MOBA_BLOCK + offs
        qp = qpos[None, None, :, None, None]
        logits = jnp.einsum('bqhd,bhqjsd->bhqjs', qc, k_sel).astype(f32) * scale \
            - slopes[None, :, None, None, None] * (qp - kpos).astype(f32)
        valid = ok[..., None] & (kpos <= qp)
        logits = jnp.where(valid, logits, -jnp.inf)
        p = jax.nn.softmax(logits.reshape(bsz, nh, t, -1), axis=-1).reshape(logits.shape).astype(v.dtype)
        return jnp.einsum('bhqjs,bhqjsd->bqhd', p, v_sel)

    qpos_all = pos0 + jnp.arange(tq)
    if tq <= Q_BLOCK:
        return attend(q, qpos_all)
    nq = -(-tq // Q_BLOCK)
    padq = nq * Q_BLOCK - tq
    qc = jnp.pad(q, ((0, 0), (0, padq), (0, 0), (0, 0))).reshape(bsz, nq, Q_BLOCK, nh, dh).transpose(1, 0, 2, 3, 4)
    pc = (pos0 + jnp.arange(nq * Q_BLOCK)).reshape(nq, Q_BLOCK)
    out = lax.map(lambda a: attend(a[0], a[1]), (qc, pc))
    return out.transpose(1, 0, 2, 3, 4).reshape(bsz, nq * Q_BLOCK, nh, dh)[:, :tq]


def layer(x, conv_prev, ssm_prev, k_past, v_past, slopes, g_pre, w_in, w_spatial, b_spatial, ln_v_g, ln_v_b,
          conv_w, conv_b, dt_bias, a_log, d_skip, g_ssm, w_branch, w_out, g_post):
    bsz, t, _ = x.shape
    h = rmsnorm(x, g_pre)
    proj = jnp.einsum('btd,dk->btk', h, w_in)
    (u_a, v_a, z_a, xbc, z_b, dt_raw, q, k, v, z_c, gate_logits) = jnp.split(proj, _split_points(), axis=-1)
    u_a = jax.nn.gelu(u_a)
    v_a = layernorm(jax.nn.gelu(v_a), ln_v_g, ln_v_b)
    o_a = chunk_spatial_gate(u_a, v_a, w_spatial, b_spatial) * jax.nn.silu(z_a)
    xbc, conv_new = causal_conv(xbc, conv_prev, conv_w, conv_b)
    x_s, b_s, c_s = jnp.split(xbc, [W_BRANCH, W_BRANCH + SSM_GROUPS * SSM_STATE], axis=-1)
    x_s = x_s.reshape(bsz, t, SSM_HEADS, SSM_HEAD_DIM)
    dt = jax.nn.softplus(dt_raw.astype(jnp.float32) + dt_bias.astype(jnp.float32))
    y_s, ssm_new = ssd_scan(x_s, dt, -jnp.exp(a_log.astype(jnp.float32)),
                            b_s.reshape(bsz, t, SSM_GROUPS, SSM_STATE),
                            c_s.reshape(bsz, t, SSM_GROUPS, SSM_STATE), ssm_prev)
    y_s = y_s + d_skip[:, None] * x_s
    o_b = rmsnorm(y_s.reshape(bsz, t, W_BRANCH) * jax.nn.silu(z_b), g_ssm)
    q = q.reshape(bsz, t, ATT_HEADS, ATT_HEAD_DIM)
    k = k.reshape(bsz, t, ATT_HEADS, ATT_HEAD_DIM)
    v = v.reshape(bsz, t, ATT_HEADS, ATT_HEAD_DIM)
    att = moba_attention(q, jnp.concatenate([k_past.astype(k.dtype), k], axis=1),
                         jnp.concatenate([v_past.astype(v.dtype), v], axis=1), slopes)
    o_c = att.reshape(bsz, t, W_BRANCH) * jax.nn.silu(z_c)
    branches = jnp.stack([o_a, o_b, o_c], axis=2)
    proj_br = jnp.einsum('btrk,rkd->btrd', branches, w_branch)
    gates = jax.nn.sigmoid(gate_logits.reshape(bsz, t, N_BRANCH, D_MODEL))
    merged = jnp.sum(gates * proj_br, axis=2)
    out = jnp.einsum('btd,de->bte', merged, w_out)
    return x + rmsnorm(out, g_post), conv_new, ssm_new, k, v, v_a


def setup_inputs(seed: int = 0) -> dict:
    key = jax.random.key(seed)
    ks = jax.random.split(key, 24)
    f32 = jnp.float32
    n_pages = PAST_LEN // PAGE_SIZE
    n_phys = (DEC_BATCH * n_pages * 5) // 4
    nrm = lambda kk, shape, s=1.0: s * jax.random.normal(kk, shape, f32)
    x_prompt = nrm(ks[0], (BATCH, SEQ, D_MODEL))
    x_sample = nrm(ks[1], (DEC_BATCH, DEC_SEQ, D_MODEL))
    cache_k = nrm(ks[2], (DEPTH, n_phys, PAGE_SIZE, ATT_HEADS, ATT_HEAD_DIM))
    cache_v = nrm(ks[3], (DEPTH, n_phys, PAGE_SIZE, ATT_HEADS, ATT_HEAD_DIM))
    state_ssm = nrm(ks[4], (DEPTH, DEC_BATCH, SSM_HEADS, SSM_HEAD_DIM, SSM_STATE), 0.5)
    state_conv = nrm(ks[5], (DEPTH, DEC_BATCH, SSM_CONV - 1, CONV_DIM))
    perm = jax.random.permutation(ks[6], n_phys)
    page_table = perm[:DEC_BATCH * n_pages].reshape(DEC_BATCH, n_pages).astype(jnp.int32)
    g_pre = 1.0 + nrm(ks[7], (DEPTH, D_MODEL), 0.1)
    w_in = nrm(ks[8], (DEPTH, D_MODEL, IN_COLS), D_MODEL ** -0.5)
    w_spatial = nrm(ks[9], (DEPTH, GMLP_GROUPS, GMLP_CHUNK, GMLP_CHUNK), GMLP_CHUNK ** -0.5)
    b_spatial = 1.0 + nrm(ks[10], (DEPTH, GMLP_GROUPS, GMLP_CHUNK), 0.1)
    ln_v_g = 1.0 + nrm(ks[11], (DEPTH, W_BRANCH), 0.1)
    ln_v_b = nrm(ks[12], (DEPTH, W_BRANCH), 0.02)
    conv_w = nrm(ks[13], (DEPTH, SSM_CONV, CONV_DIM), SSM_CONV ** -0.5)
    conv_b = nrm(ks[14], (DEPTH, CONV_DIM), 0.02)
    dt0 = jnp.exp(jax.random.uniform(ks[15], (DEPTH, SSM_HEADS), f32, math.log(1e-3), math.log(1e-1)))
    dt_bias = dt0 + jnp.log(-jnp.expm1(-dt0))
    a_log = jnp.log(jax.random.uniform(ks[16], (DEPTH, SSM_HEADS), f32, 1.0, 16.0))
    d_skip = 1.0 + nrm(ks[17], (DEPTH, SSM_HEADS), 0.1)
    g_ssm = 1.0 + nrm(ks[18], (DEPTH, W_BRANCH), 0.1)
    w_branch = nrm(ks[19], (DEPTH, N_BRANCH, W_BRANCH, D_MODEL), W_BRANCH ** -0.5)
    w_out = nrm(ks[20], (DEPTH, D_MODEL, D_MODEL), D_MODEL ** -0.5)
    g_post = 1.0 + nrm(ks[21], (DEPTH, D_MODEL), 0.1)
    return {'x_prompt': x_prompt, 'x_sample': x_sample, 'cache_k': cache_k, 'cache_v': cache_v,
            'state_ssm': state_ssm, 'state_conv': state_conv, 'page_table': page_table,
            'g_pre': g_pre, 'w_in': w_in, 'w_spatial': w_spatial, 'b_spatial': b_spatial,
            'ln_v_g': ln_v_g, 'ln_v_b': ln_v_b, 'conv_w': conv_w, 'conv_b': conv_b,
            'dt_bias': dt_bias, 'a_log': a_log, 'd_skip': d_skip, 'g_ssm': g_ssm,
            'w_branch': w_branch, 'w_out': w_out, 'g_post': g_post}


def reference(x_prompt, x_sample, cache_k, cache_v, state_ssm, state_conv, page_table,
              g_pre, w_in, w_spatial, b_spatial, ln_v_g, ln_v_b, conv_w, conv_b,
              dt_bias, a_log, d_skip, g_ssm, w_branch, w_out, g_post):
    slopes = alibi_slopes(ATT_HEADS)
    bp = x_prompt.shape[0]
    bs = x_sample.shape[0]
    xp, xs = x_prompt, x_sample
    kp_l, vp_l, sp_l, cp_l = [], [], [], []
    ks_l, vs_l, ss_l, cs_l, gv_l = [], [], [], [], []
    for l in range(DEPTH):
        lw = (g_pre[l], w_in[l], w_spatial[l], b_spatial[l], ln_v_g[l], ln_v_b[l], conv_w[l], conv_b[l],
              dt_bias[l], a_log[l], d_skip[l], g_ssm[l], w_branch[l], w_out[l], g_post[l])
        conv0 = jnp.zeros((bp, SSM_CONV - 1, CONV_DIM), xp.dtype)
        ssm0 = jnp.zeros((bp, SSM_HEADS, SSM_HEAD_DIM, SSM_STATE), jnp.float32)
        kv0 = jnp.zeros((bp, 0, ATT_HEADS, ATT_HEAD_DIM), xp.dtype)
        xp, cpn, spn, kpn, vpn, _ = layer(xp, conv0, ssm0, kv0, kv0, slopes, *lw)
        k_past = cache_k[l][page_table].reshape(bs, -1, ATT_HEADS, ATT_HEAD_DIM)
        v_past = cache_v[l][page_table].reshape(bs, -1, ATT_HEADS, ATT_HEAD_DIM)
        xs, csn, ssn, ksn, vsn, gvn = layer(xs, state_conv[l], state_ssm[l], k_past, v_past, slopes, *lw)
        kp_l.append(kpn); vp_l.append(vpn); sp_l.append(spn); cp_l.append(cpn)
        ks_l.append(ksn); vs_l.append(vsn); ss_l.append(ssn); cs_l.append(csn); gv_l.append(gvn)
    return (xp, xs, jnp.stack(kp_l), jnp.stack(vp_l), jnp.stack(sp_l), jnp.stack(cp_l),
            jnp.stack(ks_l), jnp.stack(vs_l), jnp.stack(ss_l), jnp.stack(cs_l), jnp.stack(gv_l))
```

```python
import functools

import jax
import jax.numpy as jnp
import numpy as np
from jax import lax
from jax.experimental import pallas as pl
from jax.experimental.pallas import tpu as pltpu

F32 = jnp.float32
BF16 = jnp.bfloat16

D_MODEL = 2048
W_BRANCH = 1024
N_BRANCH = 3
GMLP_GROUPS = 4
GMLP_CHUNK = 128
GMLP_GW = W_BRANCH // GMLP_GROUPS
SSM_HEAD_DIM = 64
SSM_HEADS = 16
SSM_GROUPS = 2
SSM_STATE = 128
SSM_CONV = 4
SSM_CHUNK = 256
CONV_DIM = W_BRANCH + 2 * SSM_GROUPS * SSM_STATE
ATT_HEAD_DIM = 64
ATT_HEADS = 16
MOBA_BLOCK = 256
MOBA_TOPK = 3
PAGE_SIZE = 128
EPS = 1e-6
IN_SIZES = [W_BRANCH, W_BRANCH, W_BRANCH, CONV_DIM, W_BRANCH, SSM_HEADS,
            W_BRANCH, W_BRANCH, W_BRANCH, W_BRANCH, N_BRANCH * D_MODEL]

LANES = 128
SUBLANES = 8
VMEM_LIMIT = 56 * 1024 * 1024

COL_U, COL_V, COL_ZA, COL_ZB, COL_Q, COL_K, COL_VA, COL_ZC = range(8)
GATE_COLBLK = 4
XD_COLBLK = 7
XD_WIDTH = D_MODEL
N_PROJ = 8 * W_BRANCH + N_BRANCH * D_MODEL + XD_WIDTH
NEG = -1e30


def _params(*sem):
    return pltpu.CompilerParams(dimension_semantics=sem, vmem_limit_bytes=VMEM_LIMIT)


def _sigmoid(x):
    return 1.0 / (1.0 + jnp.exp(-x))


def _silu(x):
    return x * _sigmoid(x)


def _gelu(x):
    return 0.5 * x * (1.0 + jnp.tanh(0.7978845608028654 * (x + 0.044715 * (x * x * x))))


def _softplus(x):
    return jnp.maximum(x, 0.0) + jnp.log1p(jnp.exp(-jnp.abs(x)))


def _dot(a, b):
    return jnp.dot(a, b, preferred_element_type=F32)


def _dot_nt(a, b):
    return lax.dot_general(a, b, (((1,), (1,)), ((), ())), preferred_element_type=F32)


def _split3(x):
    x1 = x.astype(BF16)
    r1 = x - x1.astype(F32)
    x2 = r1.astype(BF16)
    r2 = r1 - x2.astype(F32)
    return x1, x2, r2.astype(BF16)


def _prenorm_kernel(x_ref, g_ref, o_ref):
    x = x_ref[...]
    ms = jnp.mean(x * x, axis=-1, keepdims=True)
    o_ref[...] = (x * lax.rsqrt(ms + EPS) * g_ref[...]).astype(o_ref.dtype)


def _prenorm(x, g, tm):
    m, d = x.shape
    return pl.pallas_call(
        _prenorm_kernel,
        out_shape=jax.ShapeDtypeStruct((m, d), BF16),
        grid=(m // tm,),
        in_specs=[pl.BlockSpec((tm, d), lambda i: (i, 0)),
                  pl.BlockSpec((1, d), lambda i: (0, 0))],
        out_specs=pl.BlockSpec((tm, d), lambda i: (i, 0)),
        compiler_params=_params("parallel"),
        name="prenorm",
    )(x, g.reshape(1, d))


def _mm_kernel(x_ref, w_ref, o_ref):
    o_ref[...] = _dot(x_ref[...], w_ref[...])


def _inproj(h, w, tm, tn):
    m, k = h.shape
    n = w.shape[1]
    return pl.pallas_call(
        _mm_kernel,
        out_shape=jax.ShapeDtypeStruct((m, n), F32),
        grid=(n // tn, m // tm),
        in_specs=[pl.BlockSpec((tm, k), lambda j, i: (i, 0)),
                  pl.BlockSpec((k, tn), lambda j, i: (0, j))],
        out_specs=pl.BlockSpec((tm, tn), lambda j, i: (i, j)),
        compiler_params=_params("parallel", "parallel"),
        name="inproj",
    )(h, w)


def _layernorm_rows(x, g, b):
    mu = jnp.mean(x, axis=-1, keepdims=True)
    d = x - mu
    var = jnp.mean(d * d, axis=-1, keepdims=True)
    return d * lax.rsqrt(var + EPS) * g + b


def _gmlp_kernel(u_ref, v_ref, z_ref, ws_ref, bt_ref, lg_ref, lb_ref, o_ref, *, cc, nch):
    row = lax.broadcasted_iota(jnp.int32, (cc, cc), 0)
    col = lax.broadcasted_iota(jnp.int32, (cc, cc), 1)
    tri = row >= col
    for c in range(nch):
        sl = pl.ds(c * cc, cc)
        va = _layernorm_rows(_gelu(v_ref[sl, :]), lg_ref[...], lb_ref[...])
        for g in range(GMLP_GROUPS):
            gs = slice(g * GMLP_GW, (g + 1) * GMLP_GW)
            wc = jnp.where(tri, ws_ref[g], 0.0).astype(BF16)
            mixed = _dot(wc, va[:, gs].astype(BF16)) + bt_ref[:, g:g + 1]
            o = _gelu(u_ref[sl, gs]) * mixed * _silu(z_ref[sl, gs])
            o_ref[sl, gs] = o.astype(o_ref.dtype)


def _gmlp(proj, w_s, b_s, ln_g, ln_b, tr):
    m = proj.shape[0]
    cc = GMLP_CHUNK
    return pl.pallas_call(
        functools.partial(_gmlp_kernel, cc=cc, nch=tr // cc),
        out_shape=jax.ShapeDtypeStruct((m, W_BRANCH), BF16),
        grid=(m // tr,),
        in_specs=[pl.BlockSpec((tr, W_BRANCH), lambda i: (i, COL_U)),
                  pl.BlockSpec((tr, W_BRANCH), lambda i: (i, COL_V)),
                  pl.BlockSpec((tr, W_BRANCH), lambda i: (i, COL_ZA)),
                  pl.BlockSpec((GMLP_GROUPS, cc, cc), lambda i: (0, 0, 0)),
                  pl.BlockSpec((cc, GMLP_GROUPS), lambda i: (0, 0)),
                  pl.BlockSpec((1, W_BRANCH), lambda i: (0, 0)),
                  pl.BlockSpec((1, W_BRANCH), lambda i: (0, 0))],
        out_specs=pl.BlockSpec((tr, W_BRANCH), lambda i: (i, 0)),
        compiler_params=_params("parallel"),
        name="gmlp",
    )(proj, proj, proj, w_s, b_s.T, ln_g.reshape(1, -1), ln_b.reshape(1, -1))


def _gmlp1_kernel(u_ref, v_ref, z_ref, w0_ref, b0_ref, lg_ref, lb_ref, o_ref, va_ref):
    va = _layernorm_rows(_gelu(v_ref[...]), lg_ref[...], lb_ref[...])
    va_ref[...] = va
    mixed = w0_ref[...] * va + b0_ref[...]
    o_ref[...] = (_gelu(u_ref[...]) * mixed * _silu(z_ref[...])).astype(o_ref.dtype)


def _gmlp_single(proj, w_s, b_s, ln_g, ln_b):
    m = proj.shape[0]
    w0 = jnp.repeat(w_s[:, 0, 0], GMLP_GW).reshape(1, W_BRANCH)
    b0 = jnp.repeat(b_s[:, 0], GMLP_GW).reshape(1, W_BRANCH)
    row = lambda: pl.BlockSpec((1, W_BRANCH), lambda i: (0, 0))
    return pl.pallas_call(
        _gmlp1_kernel,
        out_shape=(jax.ShapeDtypeStruct((m, W_BRANCH), BF16),
                   jax.ShapeDtypeStruct((m, W_BRANCH), F32)),
        grid=(1,),
        in_specs=[pl.BlockSpec((m, W_BRANCH), lambda i: (0, COL_U)),
                  pl.BlockSpec((m, W_BRANCH), lambda i: (0, COL_V)),
                  pl.BlockSpec((m, W_BRANCH), lambda i: (0, COL_ZA)),
                  row(), row(), row(), row()],
        out_specs=(pl.BlockSpec((m, W_BRANCH), lambda i: (0, 0)),
                   pl.BlockSpec((m, W_BRANCH), lambda i: (0, 0))),
        compiler_params=_params("arbitrary"),
        name="gmlp_single",
    )(proj, proj, proj, w0, b0, ln_g.reshape(1, -1), ln_b.reshape(1, -1))


def _ssd_kernel(xd_ref, zb_ref, prev_ref, h0_ref, cw_ref, cb_ref, dtb_ref, alog_ref, dsk_ref, gs_ref,
                ob_ref, hfin_ref, state_scr, tail_scr, y_scr, xs_scr, *, L, t_valid, nc):
    c = pl.program_id(1)

    @pl.when(c == 0)
    def _():
        state_scr[...] = h0_ref[0]
        tail_scr[...] = prev_ref[0]

    raw = xd_ref[:, :CONV_DIM]
    tail = tail_scr[...]
    rid = lax.broadcasted_iota(jnp.int32, (SUBLANES, CONV_DIM), 0)
    acc = cb_ref[...] + raw * cw_ref[SSM_CONV - 1:SSM_CONV, :]
    for k in range(1, SSM_CONV):
        rolled = pltpu.roll(raw, k, axis=0)
        head = jnp.where(rid < k, pltpu.roll(tail, k, axis=0), rolled[:SUBLANES])
        shifted = jnp.concatenate([head, rolled[SUBLANES:]], axis=0)
        acc = acc + shifted * cw_ref[SSM_CONV - 1 - k:SSM_CONV - k, :]
    tail_scr[...] = raw[L - SUBLANES:, :]
    xbc = _silu(acc)
    x = xbc[:, :W_BRANCH]
    gn = SSM_GROUPS * SSM_STATE
    bm = xbc[:, W_BRANCH:W_BRANCH + gn].astype(BF16)
    cm = xbc[:, W_BRANCH + gn:].astype(BF16)

    dt = _softplus(xd_ref[:, CONV_DIM:CONV_DIM + LANES] + dtb_ref[...])
    if t_valid < nc * L:
        grow = c * L + lax.broadcasted_iota(jnp.int32, (L, LANES), 0)
        dt = jnp.where(grow < t_valid, dt, 0.0)
    da = dt * (-jnp.exp(alog_ref[...]))
    row = lax.broadcasted_iota(jnp.int32, (L, L), 0)
    col = lax.broadcasted_iota(jnp.int32, (L, L), 1)
    causal = row >= col
    tril = jnp.where(causal, 1.0, 0.0).astype(BF16)
    d1, d2, d3 = _split3(da)
    a_cs = _dot(tril, d1) + _dot(tril, d2) + _dot(tril, d3)
    a_cs_t = a_cs.T
    dt_t = dt.T
    hg = SSM_HEADS // SSM_GROUPS
    gw = hg * SSM_HEAD_DIM
    for g in range(SSM_GROUPS):
        bg = bm[:, g * SSM_STATE:(g + 1) * SSM_STATE]
        cg = cm[:, g * SSM_STATE:(g + 1) * SSM_STATE]
        cb = _dot_nt(cg, bg)
        st_g = state_scr[g * gw:(g + 1) * gw, :]
        y_off = _dot_nt(cg, st_g.astype(BF16))
        for hh in range(hg):
            h = g * hg + hh
            hs = slice(h * SSM_HEAD_DIM, (h + 1) * SSM_HEAD_DIM)
            ac = a_cs[:, h:h + 1]
            ar = a_cs_t[h:h + 1, :]
            seg = jnp.exp(jnp.where(causal, ac - ar, -jnp.inf))
            wm = (cb * seg * dt_t[h:h + 1, :]).astype(BF16)
            xh = x[:, hs]
            yd = _dot(wm, xh.astype(BF16))
            yo = y_off[:, hh * SSM_HEAD_DIM:(hh + 1) * SSM_HEAD_DIM] * jnp.exp(ac)
            y_scr[:, hs] = yd + yo
            a_last = a_cs[L - 1:L, h:h + 1]
            xs_scr[:, hs] = xh * (jnp.exp(a_last - ac) * dt[:, h:h + 1])
        sts = _dot(xs_scr[:, g * gw:(g + 1) * gw].T.astype(BF16), bg)
        for hh in range(hg):
            h = g * hg + hh
            rs = slice(h * SSM_HEAD_DIM, (h + 1) * SSM_HEAD_DIM)
            a_last = a_cs[L - 1:L, h:h + 1]
            state_scr[rs, :] = (jnp.exp(a_last) * state_scr[rs, :]
                                + sts[hh * SSM_HEAD_DIM:(hh + 1) * SSM_HEAD_DIM, :])

    gated = (y_scr[...] + dsk_ref[...] * x) * _silu(zb_ref[...])
    ms = jnp.mean(gated * gated, axis=-1, keepdims=True)
    ob_ref[...] = (gated * lax.rsqrt(ms + EPS) * gs_ref[...]).astype(ob_ref.dtype)

    @pl.when(c == nc - 1)
    def _():
        hfin_ref[0] = state_scr[...]


def _ssd(xd_arr, xd_colblk, zb_arr, zb_colblk, prev8, h0, conv_w, conv_b, dt_bias, a_log, d_skip, g_ssm,
         bsz, nc, L, t_valid):
    rows = bsz * nc * L
    pad16 = lambda a: jnp.pad(a, (0, LANES - SSM_HEADS)).reshape(1, LANES)
    const = lambda shape: pl.BlockSpec(shape, lambda b, c: (0,) * len(shape))
    return pl.pallas_call(
        functools.partial(_ssd_kernel, L=L, t_valid=t_valid, nc=nc),
        out_shape=(jax.ShapeDtypeStruct((rows, W_BRANCH), BF16),
                   jax.ShapeDtypeStruct((bsz, SSM_HEADS * SSM_HEAD_DIM, SSM_STATE), F32)),
        grid=(bsz, nc),
        in_specs=[pl.BlockSpec((L, XD_WIDTH), lambda b, c: (b * nc + c, xd_colblk)),
                  pl.BlockSpec((L, W_BRANCH), lambda b, c: (b * nc + c, zb_colblk)),
                  pl.BlockSpec((1, SUBLANES, CONV_DIM), lambda b, c: (b, 0, 0)),
                  pl.BlockSpec((1, SSM_HEADS * SSM_HEAD_DIM, SSM_STATE), lambda b, c: (b, 0, 0)),
                  const((SSM_CONV, CONV_DIM)), const((1, CONV_DIM)), const((1, LANES)), const((1, LANES)),
                  const((1, W_BRANCH)), const((1, W_BRANCH))],
        out_specs=(pl.BlockSpec((L, W_BRANCH), lambda b, c: (b * nc + c, 0)),
                   pl.BlockSpec((1, SSM_HEADS * SSM_HEAD_DIM, SSM_STATE), lambda b, c: (b, 0, 0))),
        scratch_shapes=[pltpu.VMEM((SSM_HEADS * SSM_HEAD_DIM, SSM_STATE), F32),
                        pltpu.VMEM((SUBLANES, CONV_DIM), F32),
                        pltpu.VMEM((L, W_BRANCH), F32),
                        pltpu.VMEM((L, W_BRANCH), F32)],
        compiler_params=_params("arbitrary", "arbitrary"),
        name="ssd",
    )(xd_arr, zb_arr, prev8, h0, conv_w, conv_b.reshape(1, -1), pad16(dt_bias), pad16(a_log),
      jnp.repeat(d_skip, SSM_HEAD_DIM).reshape(1, -1), g_ssm.reshape(1, -1))


def _select_topk(gate, n_elig, axis):
    idx = lax.broadcasted_iota(jnp.int32, gate.shape, axis).astype(F32)
    big = jnp.float32(1 << 30)
    g = jnp.where(idx < n_elig.astype(F32), gate, -jnp.inf)
    sel = jnp.zeros(gate.shape, jnp.bool_)
    for _ in range(MOBA_TOPK):
        m = jnp.max(g, axis=axis, keepdims=True)
        hit = (g == m) & (m > -jnp.inf) & (m < jnp.inf)
        first = jnp.min(jnp.where(hit, idx, big), axis=axis, keepdims=True)
        pick = idx == first
        sel = sel | pick
        g = jnp.where(pick, -jnp.inf, g)
    return sel


def _attn_kernel(q_ref, k_ref, v_ref, zc_ref, sl_ref, o_ref, kbf, vbf, km_scr, *, T, tq):
    nb = T // MOBA_BLOCK
    qb = pl.program_id(2)

    @pl.when(qb == 0)
    def _():
        k = k_ref[...]
        kbf[...] = k.astype(BF16)
        vbf[...] = v_ref[...].astype(BF16)
        km_scr[...] = jnp.zeros(km_scr.shape, F32)
        km_scr[0:nb, :] = jnp.mean(k.reshape(nb, MOBA_BLOCK, LANES), axis=1)

    row = lax.broadcasted_iota(jnp.int32, (tq, MOBA_BLOCK), 0)
    col = lax.broadcasted_iota(jnp.int32, (tq, MOBA_BLOCK), 1)
    rel = (col - row).astype(F32)
    lane = lax.broadcasted_iota(jnp.int32, (tq, LANES), 1)
    scale = ATT_HEAD_DIM ** -0.5
    for i in range(LANES // ATT_HEAD_DIM):
        hs = slice(i * ATT_HEAD_DIM, (i + 1) * ATT_HEAD_DIM)
        qh = q_ref[:, hs]
        kmh = km_scr[:, hs]
        q1, q2, _ = _split3(qh)
        k1, k2, _ = _split3(kmh)
        gate = _dot_nt(q1, k1) + _dot_nt(q1, k2) + _dot_nt(q2, k1)
        sel = _select_topk(gate, qb, 1)
        selb = jnp.where(sel, 0.0, NEG)
        qs = (qh * scale).astype(BF16)
        slope = sl_ref[0, :, i * ATT_HEAD_DIM:i * ATT_HEAD_DIM + 1]
        alibi0 = slope * rel

        def body(n, carry):
            m, l, acc = carry
            r0 = pl.multiple_of(n * MOBA_BLOCK, MOBA_BLOCK)
            kb = kbf[pl.ds(r0, MOBA_BLOCK), hs]
            vb = vbf[pl.ds(r0, MOBA_BLOCK), hs]
            colb = jnp.sum(jnp.where(lane == n, selb, 0.0), axis=1, keepdims=True)
            shift = slope * ((n - qb) * MOBA_BLOCK).astype(F32)
            s = _dot_nt(qs, kb) + (alibi0 + shift) + colb
            m_new = jnp.maximum(m, jnp.max(s, axis=1, keepdims=True))
            p = jnp.exp(s - m_new)
            alpha = jnp.exp(m - m_new)
            l = alpha * l + jnp.sum(p, axis=1, keepdims=True)
            acc = alpha * acc + _dot(p.astype(BF16), vb)
            return m_new, l, acc

        init = (jnp.full((tq, 1), NEG, F32), jnp.zeros((tq, 1), F32), jnp.zeros((tq, ATT_HEAD_DIM), F32))
        m, l, acc = lax.fori_loop(0, qb, body, init)
        r0 = pl.multiple_of(qb * MOBA_BLOCK, MOBA_BLOCK)
        kb = kbf[pl.ds(r0, MOBA_BLOCK), hs]
        vb = vbf[pl.ds(r0, MOBA_BLOCK), hs]
        s = jnp.where(col <= row, _dot_nt(qs, kb) + alibi0, NEG)
        m_new = jnp.maximum(m, jnp.max(s, axis=1, keepdims=True))
        p = jnp.exp(s - m_new)
        alpha = jnp.exp(m - m_new)
        l = alpha * l + jnp.sum(p, axis=1, keepdims=True)
        acc = alpha * acc + _dot(p.astype(BF16), vb)
        o_ref[:, hs] = ((acc / l) * _silu(zc_ref[:, hs])).astype(o_ref.dtype)


def _alibi_slopes():
    return 2.0 ** (-8.0 * jnp.arange(1, ATT_HEADS + 1, dtype=F32) / ATT_HEADS)


def _attn_prompt(proj, bsz, T):
    tq = MOBA_BLOCK
    nq = T // tq
    hp_n = W_BRANCH // LANES
    slopes = jnp.repeat(_alibi_slopes(), ATT_HEAD_DIM).reshape(hp_n, 1, LANES)
    cpb = W_BRANCH // LANES
    return pl.pallas_call(
        functools.partial(_attn_kernel, T=T, tq=tq),
        out_shape=jax.ShapeDtypeStruct((bsz * T, W_BRANCH), BF16),
        grid=(bsz, hp_n, nq),
        in_specs=[pl.BlockSpec((tq, LANES), lambda b, hp, qi: (b * nq + qi, COL_Q * cpb + hp)),
                  pl.BlockSpec((T, LANES), lambda b, hp, qi: (b, COL_K * cpb + hp)),
                  pl.BlockSpec((T, LANES), lambda b, hp, qi: (b, COL_VA * cpb + hp)),
                  pl.BlockSpec((tq, LANES), lambda b, hp, qi: (b * nq + qi, COL_ZC * cpb + hp)),
                  pl.BlockSpec((1, 1, LANES), lambda b, hp, qi: (hp, 0, 0))],
        out_specs=pl.BlockSpec((tq, LANES), lambda b, hp, qi: (b * nq + qi, hp)),
        scratch_shapes=[pltpu.VMEM((T, LANES), BF16), pltpu.VMEM((T, LANES), BF16),
                        pltpu.VMEM((LANES, LANES), F32)],
        compiler_params=_params("arbitrary", "arbitrary", "arbitrary"),
        name="attn_prompt",
    )(proj, proj, proj, proj, slopes)


PAGES_PER_STEP = 16
BLOCK_PAGES = MOBA_BLOCK // PAGE_SIZE


def _kmean_kernel(pt_ref, *refs):
    o_ref = refs[PAGES_PER_STEP]
    for i in range(PAGES_PER_STEP // BLOCK_PAGES):
        s = jnp.zeros((1, W_BRANCH), F32)
        for j in range(BLOCK_PAGES):
            s = s + jnp.sum(refs[i * BLOCK_PAGES + j][0], axis=0, keepdims=True)
        o_ref[0, i:i + 1, :] = s * (1.0 / MOBA_BLOCK)


def _kmean_paged(cache3, page_off, pt_flat, bsz, n_pages):
    steps = n_pages // PAGES_PER_STEP
    bps = PAGES_PER_STEP // BLOCK_PAGES

    def page_spec(i):
        return pl.BlockSpec((1, PAGE_SIZE, W_BRANCH),
                            lambda b, s, pt: (page_off + pt[b * n_pages + s * PAGES_PER_STEP + i], 0, 0))

    return pl.pallas_call(
        _kmean_kernel,
        out_shape=jax.ShapeDtypeStruct((bsz, n_pages // BLOCK_PAGES, W_BRANCH), F32),
        grid_spec=pltpu.PrefetchScalarGridSpec(
            num_scalar_prefetch=1,
            grid=(bsz, steps),
            in_specs=[page_spec(i) for i in range(PAGES_PER_STEP)],
            out_specs=pl.BlockSpec((1, bps, W_BRANCH), lambda b, s, pt: (b, s, 0))),
        compiler_params=_params("arbitrary", "arbitrary"),
        name="kmean_paged",
    )(pt_flat, *([cache3] * PAGES_PER_STEP))


def _gate_topk_kernel(q_ref, km_ref, o_ref, *, nblk):
    km = km_ref[0]
    prod = km * q_ref[0]
    ci = lax.broadcasted_iota(jnp.int32, (W_BRANCH, LANES), 0)
    hi = lax.broadcasted_iota(jnp.int32, (W_BRANCH, LANES), 1)
    ind = jnp.where(ci // ATT_HEAD_DIM == hi, 1.0, 0.0).astype(BF16)
    p1, p2, p3 = _split3(prod)
    gate = _dot(p1, ind) + _dot(p2, ind) + _dot(p3, ind)
    g = gate
    idx = lax.broadcasted_iota(jnp.int32, g.shape, 0).astype(F32)
    rid = lax.broadcasted_iota(jnp.int32, (SUBLANES, LANES), 0)
    big = jnp.float32(1 << 30)
    out = jnp.zeros((SUBLANES, LANES), F32)
    for r in range(MOBA_TOPK):
        m = jnp.max(g, axis=0, keepdims=True)
        first = jnp.min(jnp.where(g == m, idx, big), axis=0, keepdims=True)
        out = jnp.where(rid == r, jnp.broadcast_to(first, (SUBLANES, LANES)), out)
        g = jnp.where(idx == first, -jnp.inf, g)
    o_ref[0] = out.astype(jnp.int32)


def _gate_topk(q3, kmean, bsz):
    nblk = kmean.shape[1]
    return pl.pallas_call(
        functools.partial(_gate_topk_kernel, nblk=nblk),
        out_shape=jax.ShapeDtypeStruct((bsz, SUBLANES, LANES), jnp.int32),
        grid=(bsz,),
        in_specs=[pl.BlockSpec((1, 1, W_BRANCH), lambda b: (b, 0, 0)),
                  pl.BlockSpec((1, nblk, W_BRANCH), lambda b: (b, 0, 0))],
        out_specs=pl.BlockSpec((1, SUBLANES, LANES), lambda b: (b, 0, 0)),
        compiler_params=_params("arbitrary"),
        name="gate_topk",
    )(q3, kmean)


N_SEL_TILES = MOBA_TOPK * BLOCK_PAGES


def _attn_sample_kernel(pt_ref, idx_ref, q_ref, kn_ref, vn_ref, zc_ref, sl_ref, *refs, past_len):
    k_refs = refs[:N_SEL_TILES]
    v_refs = refs[N_SEL_TILES:2 * N_SEL_TILES]
    o_ref = refs[2 * N_SEL_TILES]
    b = pl.program_id(0)
    h = pl.program_id(1)
    q = jnp.broadcast_to(q_ref[0, 0], (SUBLANES, LANES)).astype(BF16)
    slope = sl_ref[0]
    lane = lax.broadcasted_iota(jnp.int32, (SUBLANES, PAGE_SIZE), 1)
    logits = []
    for j in range(MOBA_TOPK):
        blk = idx_ref[(b * MOBA_TOPK + j) * ATT_HEADS + h]
        for half in range(BLOCK_PAGES):
            kpos = blk * MOBA_BLOCK + half * PAGE_SIZE + lane
            s = _dot_nt(q, k_refs[j * BLOCK_PAGES + half][...].astype(BF16))
            logits.append(s - slope * (past_len - kpos).astype(F32))
    s_own = jnp.sum(q.astype(F32) * kn_ref[0, 0].astype(BF16).astype(F32), axis=1, keepdims=True)
    m = s_own
    for s in logits:
        m = jnp.maximum(m, jnp.max(s, axis=1, keepdims=True))
    p_own = jnp.exp(s_own - m)
    l = p_own
    acc = p_own.astype(BF16).astype(F32) * vn_ref[0, 0].astype(BF16).astype(F32)
    for t, s in enumerate(logits):
        p = jnp.exp(s - m)
        l = l + jnp.sum(p, axis=1, keepdims=True)
        acc = acc + _dot(p.astype(BF16), v_refs[t][...].astype(BF16))
    o = (acc / l) * _silu(zc_ref[0, 0])
    o_ref[0, 0] = o[0:1, :]


def _attn_sample(cache_k2, cache_v2, page_off, pt_flat, idx_flat, q128, kn128, vn128, zc128, bsz, n_pages):
    past_len = n_pages * PAGE_SIZE
    slopes = jnp.broadcast_to(_alibi_slopes()[:, None, None], (ATT_HEADS, 1, LANES))

    def tile_spec(j, half):
        def imap(b, h, pt, idx):
            blk = idx[(b * MOBA_TOPK + j) * ATT_HEADS + h]
            return (page_off + pt[b * n_pages + blk * BLOCK_PAGES + half], h // 2)
        return pl.BlockSpec((PAGE_SIZE, LANES), imap)

    tiles = [tile_spec(j, half) for j in range(MOBA_TOPK) for half in range(BLOCK_PAGES)]
    head = lambda: pl.BlockSpec((1, 1, 1, LANES), lambda b, h, pt, idx: (b, h, 0, 0))
    return pl.pallas_call(
        functools.partial(_attn_sample_kernel, past_len=past_len),
        out_shape=jax.ShapeDtypeStruct((bsz, ATT_HEADS, 1, LANES), F32),
        grid_spec=pltpu.PrefetchScalarGridSpec(
            num_scalar_prefetch=2,
            grid=(bsz, ATT_HEADS),
            in_specs=[head(), head(), head(), head(),
                      pl.BlockSpec((1, 1, LANES), lambda b, h, pt, idx: (h, 0, 0))] + tiles + tiles,
            out_specs=head()),
        compiler_params=_params("arbitrary", "arbitrary"),
        name="attn_sample",
    )(pt_flat, idx_flat, q128, kn128, vn128, zc128, slopes,
      *([cache_k2] * N_SEL_TILES), *([cache_v2] * N_SEL_TILES))


def _merge_kernel(oa_ref, ob_ref, oc_ref, ga_ref, gb_ref, gc_ref, w_ref, o_ref):
    acc = _sigmoid(ga_ref[...]) * _dot(oa_ref[...], w_ref[0])
    acc = acc + _sigmoid(gb_ref[...]) * _dot(ob_ref[...], w_ref[1])
    acc = acc + _sigmoid(gc_ref[...]) * _dot(oc_ref[...], w_ref[2])
    o_ref[...] = acc.astype(o_ref.dtype)


def _merge(o_a, o_b, o_c, proj, w_branch, tm, tn):
    m = o_a.shape[0]
    nt = D_MODEL // tn
    gate_blk0 = GATE_COLBLK * D_MODEL // tn
    br = lambda: pl.BlockSpec((tm, W_BRANCH), lambda j, i: (i, 0))
    gate = lambda r: pl.BlockSpec((tm, tn), lambda j, i: (i, gate_blk0 + r * nt + j))
    return pl.pallas_call(
        _merge_kernel,
        out_shape=jax.ShapeDtypeStruct((m, D_MODEL), BF16),
        grid=(nt, m // tm),
        in_specs=[br(), br(), br(), gate(0), gate(1), gate(2),
                  pl.BlockSpec((N_BRANCH, W_BRANCH, tn), lambda j, i: (0, 0, j))],
        out_specs=pl.BlockSpec((tm, tn), lambda j, i: (i, j)),
        compiler_params=_params("parallel", "parallel"),
        name="merge",
    )(o_a, o_b, o_c, proj, proj, proj, w_branch)


def _outproj_kernel(m_ref, w_ref, x_ref, g_ref, o_ref):
    out = _dot(m_ref[...], w_ref[...])
    ms = jnp.mean(out * out, axis=-1, keepdims=True)
    o_ref[...] = x_ref[...] + out * lax.rsqrt(ms + EPS) * g_ref[...]


def _outproj(merged, w_out, x, g_post, tm):
    m = merged.shape[0]
    return pl.pallas_call(
        _outproj_kernel,
        out_shape=jax.ShapeDtypeStruct((m, D_MODEL), F32),
        grid=(m // tm,),
        in_specs=[pl.BlockSpec((tm, D_MODEL), lambda i: (i, 0)),
                  pl.BlockSpec((D_MODEL, D_MODEL), lambda i: (0, 0)),
                  pl.BlockSpec((tm, D_MODEL), lambda i: (i, 0)),
                  pl.BlockSpec((1, D_MODEL), lambda i: (0, 0))],
        out_specs=pl.BlockSpec((tm, D_MODEL), lambda i: (i, 0)),
        compiler_params=_params("parallel"),
        name="outproj",
    )(merged, w_out, x, g_post.reshape(1, -1))


def _reorder_w_in(w):
    u, v, za, xbc, zb, dt, q, k, va, zc, gl = jnp.split(w, [int(i) for i in np.cumsum(IN_SIZES)[:-1]], axis=1)
    pad = jnp.zeros((w.shape[0], XD_WIDTH - CONV_DIM - SSM_HEADS), w.dtype)
    return jnp.concatenate([u, v, za, zb, q, k, va, zc, gl, xbc, dt, pad], axis=1).astype(BF16)


def _seg(proj, blk, rows=None):
    out = proj[:, blk * W_BRANCH:(blk + 1) * W_BRANCH]
    return out if rows is None else out[:rows]


def _prompt_layer(x, bsz, T, lw):
    (g_pre, w_in, w_s, b_s, ln_g, ln_b, conv_w, conv_b, dt_bias, a_log, d_skip, g_ssm, w_br, w_out, g_post) = lw
    m = bsz * T
    h = _prenorm(x, g_pre, 256)
    proj = _inproj(h, w_in, 512, 1024)
    o_a = _gmlp(proj, w_s, b_s, ln_g, ln_b, 512)
    L = min(SSM_CHUNK, T)
    nc = T // L
    prev8 = jnp.zeros((bsz, SUBLANES, CONV_DIM), F32)
    h0 = jnp.zeros((bsz, SSM_HEADS * SSM_HEAD_DIM, SSM_STATE), F32)
    o_b, h_fin = _ssd(proj, XD_COLBLK, proj, COL_ZB, prev8, h0, conv_w, conv_b, dt_bias, a_log, d_skip, g_ssm,
                      bsz, nc, L, T)
    o_c = _attn_prompt(proj, bsz, T)
    merged = _merge(o_a, o_b, o_c, proj, w_br, 512, 1024)
    y = _outproj(merged, w_out, x, g_post, 256)
    k_new = _seg(proj, COL_K).reshape(bsz, T, ATT_HEADS, ATT_HEAD_DIM)
    v_new = _seg(proj, COL_VA).reshape(bsz, T, ATT_HEADS, ATT_HEAD_DIM)
    xbc_raw = proj[:, XD_COLBLK * XD_WIDTH:XD_COLBLK * XD_WIDTH + CONV_DIM].reshape(bsz, T, CONV_DIM)
    conv_new = xbc_raw[:, T - (SSM_CONV - 1):]
    ssm_new = h_fin.reshape(bsz, SSM_HEADS, SSM_HEAD_DIM, SSM_STATE)
    return y, k_new, v_new, ssm_new, conv_new


def _half_lanes(a):
    bsz = a.shape[0]
    a4 = a.reshape(bsz, ATT_HEADS // 2, 2, 1, ATT_HEAD_DIM)
    z = jnp.zeros_like(a4[:, :, 0])
    even = jnp.concatenate([a4[:, :, 0], z], axis=-1)
    odd = jnp.concatenate([z, a4[:, :, 1]], axis=-1)
    return jnp.stack([even, odd], axis=2).reshape(bsz, ATT_HEADS, 1, LANES)


def _unhalf_lanes(o):
    bsz = o.shape[0]
    o5 = o.reshape(bsz, ATT_HEADS // 2, 2, 2, ATT_HEAD_DIM)
    return jnp.stack([o5[:, :, 0, 0], o5[:, :, 1, 1]], axis=2).reshape(bsz, W_BRANCH)


SAMPLE_ROWS = 16
SAMPLE_L = 128


def _sample_layer(x16, bsz, conv_prev, ssm_prev, cache_k, cache_v, page_off, pt_flat, n_pages, lw):
    (g_pre, w_in, w_s, b_s, ln_g, ln_b, conv_w, conv_b, dt_bias, a_log, d_skip, g_ssm, w_br, w_out, g_post) = lw
    h = _prenorm(x16, g_pre, SAMPLE_ROWS)
    proj = _inproj(h, w_in, SAMPLE_ROWS, 1024)
    o_a, v_a = _gmlp_single(proj, w_s, b_s, ln_g, ln_b)
    xd = proj[:bsz, XD_COLBLK * XD_WIDTH:]
    xd_rows = jnp.zeros((bsz, SAMPLE_L, XD_WIDTH), F32).at[:, 0].set(xd).reshape(bsz * SAMPLE_L, XD_WIDTH)
    zb_rows = jnp.zeros((bsz, SAMPLE_L, W_BRANCH), F32).at[:, 0].set(_seg(proj, COL_ZB, bsz))
    zb_rows = zb_rows.reshape(bsz * SAMPLE_L, W_BRANCH)
    prev8 = jnp.concatenate([jnp.zeros((bsz, SUBLANES - (SSM_CONV - 1), CONV_DIM), F32), conv_prev], axis=1)
    h0 = ssm_prev.reshape(bsz, SSM_HEADS * SSM_HEAD_DIM, SSM_STATE)
    ob_rows, h_fin = _ssd(xd_rows, 0, zb_rows, 0, prev8, h0, conv_w, conv_b, dt_bias, a_log, d_skip, g_ssm,
                          bsz, 1, SAMPLE_L, 1)
    o_b = ob_rows.reshape(bsz, SAMPLE_L, W_BRANCH)[:, 0]
    o_b = jnp.pad(o_b, ((0, SAMPLE_ROWS - bsz), (0, 0)))
    conv_new = jnp.concatenate([conv_prev[:, 1:], xd[:, None, :CONV_DIM]], axis=1)
    ssm_new = h_fin.reshape(bsz, SSM_HEADS, SSM_HEAD_DIM, SSM_STATE)
    q = _seg(proj, COL_Q, bsz)
    k_new = _seg(proj, COL_K, bsz)
    v_new = _seg(proj, COL_VA, bsz)
    cache_k3 = cache_k.reshape(-1, PAGE_SIZE, W_BRANCH)
    kmean = _kmean_paged(cache_k3, page_off, pt_flat, bsz, n_pages)
    top = _gate_topk(q.reshape(bsz, 1, W_BRANCH), kmean, bsz)
    idx_flat = top[:, :MOBA_TOPK, :ATT_HEADS].reshape(-1)
    att = _attn_sample(cache_k.reshape(-1, W_BRANCH), cache_v.reshape(-1, W_BRANCH), page_off, pt_flat, idx_flat,
                       _half_lanes(q * ATT_HEAD_DIM ** -0.5), _half_lanes(k_new), _half_lanes(v_new),
                       _half_lanes(_seg(proj, COL_ZC, bsz)), bsz, n_pages)
    o_c = jnp.pad(_unhalf_lanes(att), ((0, SAMPLE_ROWS - bsz), (0, 0))).astype(BF16)
    merged = _merge(o_a, o_b, o_c, proj, w_br, SAMPLE_ROWS, 1024)
    y = _outproj(merged, w_out, x16, g_post, SAMPLE_ROWS)
    shape4 = (bsz, 1, ATT_HEADS, ATT_HEAD_DIM)
    return (y, k_new.reshape(shape4), v_new.reshape(shape4), ssm_new, conv_new,
            v_a[:bsz].reshape(bsz, 1, W_BRANCH))


def kernel(x_prompt, x_sample, cache_k, cache_v, state_ssm, state_conv, page_table, g_pre, w_in, w_spatial,
           b_spatial, ln_v_g, ln_v_b, conv_w, conv_b, dt_bias, a_log, d_skip, g_ssm, w_branch, w_out, g_post):
    depth = w_in.shape[0]
    bp, T, _ = x_prompt.shape
    bs = x_sample.shape[0]
    n_phys = cache_k.shape[1]
    n_pages = page_table.shape[1]
    pt_flat = page_table.reshape(-1).astype(jnp.int32)
    xp = x_prompt.reshape(bp * T, D_MODEL)
    xs = jnp.pad(x_sample.reshape(bs, D_MODEL), ((0, SAMPLE_ROWS - bs), (0, 0)))
    outs = [[] for _ in range(9)]
    for l in range(depth):
        lw = (g_pre[l], _reorder_w_in(w_in[l]), w_spatial[l], b_spatial[l], ln_v_g[l], ln_v_b[l], conv_w[l],
              conv_b[l], dt_bias[l], a_log[l], d_skip[l], g_ssm[l], w_branch[l].astype(BF16),
              w_out[l].astype(BF16), g_post[l])
        xp, kp, vp, sp, cp = _prompt_layer(xp, bp, T, lw)
        xs, ks, vs, ss, cs, gv = _sample_layer(xs, bs, state_conv[l], state_ssm[l], cache_k, cache_v,
                                               l * n_phys, pt_flat, n_pages, lw)
        for lst, val in zip(outs, (kp, vp, sp, cp, ks, vs, ss, cs, gv)):
            lst.append(val)
    return (xp.reshape(bp, T, D_MODEL), xs[:bs].reshape(bs, 1, D_MODEL)) + tuple(jnp.stack(o) for o in outs)
```

```python
import functools

import jax
import jax.numpy as jnp
import numpy as np
from jax import lax
from jax.experimental import pallas as pl
from jax.experimental.pallas import tpu as pltpu

F32 = jnp.float32
BF16 = jnp.bfloat16

D_MODEL = 2048
W_BRANCH = 1024
N_BRANCH = 3
GMLP_GROUPS = 4
GMLP_CHUNK = 128
GMLP_GW = W_BRANCH // GMLP_GROUPS
SSM_HEAD_DIM = 64
SSM_HEADS = 16
SSM_GROUPS = 2
SSM_STATE = 128
SSM_CONV = 4
SSM_CHUNK = 256
CONV_DIM = W_BRANCH + 2 * SSM_GROUPS * SSM_STATE
ATT_HEAD_DIM = 64
ATT_HEADS = 16
MOBA_BLOCK = 256
MOBA_TOPK = 3
PAGE_SIZE = 128
EPS = 1e-6
IN_SIZES = [W_BRANCH, W_BRANCH, W_BRANCH, CONV_DIM, W_BRANCH, SSM_HEADS,
            W_BRANCH, W_BRANCH, W_BRANCH, W_BRANCH, N_BRANCH * D_MODEL]

LANES = 128
SUBLANES = 8
VMEM_LIMIT = 56 * 1024 * 1024

COL_U, COL_V, COL_ZA, COL_ZB, COL_Q, COL_K, COL_VA, COL_ZC = range(8)
GATE_COLBLK = 4
XD_COLBLK = 7
XD_WIDTH = D_MODEL
N_PROJ = 8 * W_BRANCH + N_BRANCH * D_MODEL + XD_WIDTH
NEG = -1e30


def _params(*sem):
    return pltpu.CompilerParams(dimension_semantics=sem, vmem_limit_bytes=VMEM_LIMIT)


def _sigmoid(x):
    return 1.0 / (1.0 + jnp.exp(-x))


def _silu(x):
    return x * _sigmoid(x)


def _gelu(x):
    return 0.5 * x * (1.0 + jnp.tanh(0.7978845608028654 * (x + 0.044715 * (x * x * x))))


def _softplus(x):
    return jnp.maximum(x, 0.0) + jnp.log1p(jnp.exp(-jnp.abs(x)))


def _dot(a, b):
    return jnp.dot(a, b, preferred_element_type=F32)


def _dot_nt(a, b):
    return lax.dot_general(a, b, (((1,), (1,)), ((), ())), preferred_element_type=F32)


def _split3(x):
    x1 = x.astype(BF16)
    r1 = x - x1.astype(F32)
    x2 = r1.astype(BF16)
    r2 = r1 - x2.astype(F32)
    return x1, x2, r2.astype(BF16)


def _prenorm_kernel(x_ref, g_ref, o_ref):
    x = x_ref[...]
    ms = jnp.mean(x * x, axis=-1, keepdims=True)
    o_ref[...] = (x * lax.rsqrt(ms + EPS) * g_ref[...]).astype(o_ref.dtype)


def _prenorm(x, g, tm):
    m, d = x.shape
    return pl.pallas_call(
        _prenorm_kernel,
        out_shape=jax.ShapeDtypeStruct((m, d), BF16),
        grid=(m // tm,),
        in_specs=[pl.BlockSpec((tm, d), lambda i: (i, 0)),
                  pl.BlockSpec((1, d), lambda i: (0, 0))],
        out_specs=pl.BlockSpec((tm, d), lambda i: (i, 0)),
        compiler_params=_params("parallel"),
        name="prenorm",
    )(x, g.reshape(1, d))


def _mm_kernel(x_ref, w_ref, o_ref):
    o_ref[...] = _dot_nt(x_ref[...], w_ref[...])


def _inproj(h, w_t, tm, tn):
    m, k = h.shape
    n = w_t.shape[0]
    return pl.pallas_call(
        _mm_kernel,
        out_shape=jax.ShapeDtypeStruct((m, n), F32),
        grid=(n // tn, m // tm),
        in_specs=[pl.BlockSpec((tm, k), lambda j, i: (i, 0)),
                  pl.BlockSpec((tn, k), lambda j, i: (j, 0))],
        out_specs=pl.BlockSpec((tm, tn), lambda j, i: (i, j)),
        compiler_params=_params("parallel", "parallel"),
        name="inproj",
    )(h, w_t)


def _layernorm_rows(x, g, b):
    mu = jnp.mean(x, axis=-1, keepdims=True)
    d = x - mu
    var = jnp.mean(d * d, axis=-1, keepdims=True)
    return d * lax.rsqrt(var + EPS) * g + b


def _gmlp_kernel(u_ref, v_ref, z_ref, ws_ref, bt_ref, lg_ref, lb_ref, o_ref, *, cc, nch):
    row = lax.broadcasted_iota(jnp.int32, (cc, cc), 0)
    col = lax.broadcasted_iota(jnp.int32, (cc, cc), 1)
    tri = row >= col
    for c in range(nch):
        sl = pl.ds(c * cc, cc)
        va = _layernorm_rows(_gelu(v_ref[sl, :]), lg_ref[...], lb_ref[...])
        for g in range(GMLP_GROUPS):
            gs = slice(g * GMLP_GW, (g + 1) * GMLP_GW)
            wc = jnp.where(tri, ws_ref[g], 0.0).astype(BF16)
            mixed = _dot(wc, va[:, gs].astype(BF16)) + bt_ref[:, g:g + 1]
            o = _gelu(u_ref[sl, gs]) * mixed * _silu(z_ref[sl, gs])
            o_ref[sl, gs] = o.astype(o_ref.dtype)


def _gmlp(proj, w_s, b_s, ln_g, ln_b, tr):
    m = proj.shape[0]
    cc = GMLP_CHUNK
    return pl.pallas_call(
        functools.partial(_gmlp_kernel, cc=cc, nch=tr // cc),
        out_shape=jax.ShapeDtypeStruct((m, W_BRANCH), BF16),
        grid=(m // tr,),
        in_specs=[pl.BlockSpec((tr, W_BRANCH), lambda i: (i, COL_U)),
                  pl.BlockSpec((tr, W_BRANCH), lambda i: (i, COL_V)),
                  pl.BlockSpec((tr, W_BRANCH), lambda i: (i, COL_ZA)),
                  pl.BlockSpec((GMLP_GROUPS, cc, cc), lambda i: (0, 0, 0)),
                  pl.BlockSpec((cc, GMLP_GROUPS), lambda i: (0, 0)),
                  pl.BlockSpec((1, W_BRANCH), lambda i: (0, 0)),
                  pl.BlockSpec((1, W_BRANCH), lambda i: (0, 0))],
        out_specs=pl.BlockSpec((tr, W_BRANCH), lambda i: (i, 0)),
        compiler_params=_params("parallel"),
        name="gmlp",
    )(proj, proj, proj, w_s, b_s.T, ln_g.reshape(1, -1), ln_b.reshape(1, -1))


def _gmlp1_kernel(u_ref, v_ref, z_ref, w0_ref, b0_ref, lg_ref, lb_ref, o_ref, va_ref):
    va = _layernorm_rows(_gelu(v_ref[...]), lg_ref[...], lb_ref[...])
    va_ref[...] = va
    mixed = w0_ref[...] * va + b0_ref[...]
    o_ref[...] = (_gelu(u_ref[...]) * mixed * _silu(z_ref[...])).astype(o_ref.dtype)


def _gmlp_single(proj, w_s, b_s, ln_g, ln_b):
    m = proj.shape[0]
    w0 = jnp.repeat(w_s[:, 0, 0], GMLP_GW).reshape(1, W_BRANCH)
    b0 = jnp.repeat(b_s[:, 0], GMLP_GW).reshape(1, W_BRANCH)
    row = lambda: pl.BlockSpec((1, W_BRANCH), lambda i: (0, 0))
    return pl.pallas_call(
        _gmlp1_kernel,
        out_shape=(jax.ShapeDtypeStruct((m, W_BRANCH), BF16),
                   jax.ShapeDtypeStruct((m, W_BRANCH), F32)),
        grid=(1,),
        in_specs=[pl.BlockSpec((m, W_BRANCH), lambda i: (0, COL_U)),
                  pl.BlockSpec((m, W_BRANCH), lambda i: (0, COL_V)),
                  pl.BlockSpec((m, W_BRANCH), lambda i: (0, COL_ZA)),
                  row(), row(), row(), row()],
        out_specs=(pl.BlockSpec((m, W_BRANCH), lambda i: (0, 0)),
                   pl.BlockSpec((m, W_BRANCH), lambda i: (0, 0))),
        compiler_params=_params("arbitrary"),
        name="gmlp_single",
    )(proj, proj, proj, w0, b0, ln_g.reshape(1, -1), ln_b.reshape(1, -1))


def _ssd_kernel(xd_ref, zb_ref, prev_ref, h0_ref, cw_ref, cb_ref, dtb_ref, alog_ref, dsk_ref, gs_ref,
                ob_ref, hfin_ref, state_scr, tail_scr, y_scr, xs_scr, *, L, t_valid, nc):
    c = pl.program_id(1)

    @pl.when(c == 0)
    def _():
        state_scr[...] = h0_ref[0]
        tail_scr[...] = prev_ref[0]

    raw = xd_ref[:, :CONV_DIM]
    tail = tail_scr[...]
    rid = lax.broadcasted_iota(jnp.int32, (SUBLANES, CONV_DIM), 0)
    acc = cb_ref[...] + raw * cw_ref[SSM_CONV - 1:SSM_CONV, :]
    for k in range(1, SSM_CONV):
        rolled = pltpu.roll(raw, k, axis=0)
        head = jnp.where(rid < k, pltpu.roll(tail, k, axis=0), rolled[:SUBLANES])
        shifted = jnp.concatenate([head, rolled[SUBLANES:]], axis=0)
        acc = acc + shifted * cw_ref[SSM_CONV - 1 - k:SSM_CONV - k, :]
    tail_scr[...] = raw[L - SUBLANES:, :]
    xbc = _silu(acc)
    x = xbc[:, :W_BRANCH]
    gn = SSM_GROUPS * SSM_STATE
    bm = xbc[:, W_BRANCH:W_BRANCH + gn].astype(BF16)
    cm = xbc[:, W_BRANCH + gn:].astype(BF16)

    dt = _softplus(xd_ref[:, CONV_DIM:CONV_DIM + LANES] + dtb_ref[...])
    if t_valid < nc * L:
        grow = c * L + lax.broadcasted_iota(jnp.int32, (L, LANES), 0)
        dt = jnp.where(grow < t_valid, dt, 0.0)
    da = dt * (-jnp.exp(alog_ref[...]))
    row = lax.broadcasted_iota(jnp.int32, (L, L), 0)
    col = lax.broadcasted_iota(jnp.int32, (L, L), 1)
    causal = row >= col
    tril = jnp.where(causal, 1.0, 0.0).astype(BF16)
    d1, d2, d3 = _split3(da)
    a_cs = _dot(tril, d1) + _dot(tril, d2) + _dot(tril, d3)
    a_cs_t = a_cs.T
    dt_t = dt.T
    hg = SSM_HEADS // SSM_GROUPS
    gw = hg * SSM_HEAD_DIM
    for g in range(SSM_GROUPS):
        bg = bm[:, g * SSM_STATE:(g + 1) * SSM_STATE]
        cg = cm[:, g * SSM_STATE:(g + 1) * SSM_STATE]
        cb = _dot_nt(cg, bg)
        st_g = state_scr[g * gw:(g + 1) * gw, :]
        y_off = _dot_nt(cg, st_g.astype(BF16))
        for hh in range(hg):
            h = g * hg + hh
            hs = slice(h * SSM_HEAD_DIM, (h + 1) * SSM_HEAD_DIM)
            ac = a_cs[:, h:h + 1]
            ar = a_cs_t[h:h + 1, :]
            seg = jnp.exp(jnp.where(causal, ac - ar, -jnp.inf))
            wm = (cb * seg * dt_t[h:h + 1, :]).astype(BF16)
            xh = x[:, hs]
            yd = _dot(wm, xh.astype(BF16))
            yo = y_off[:, hh * SSM_HEAD_DIM:(hh + 1) * SSM_HEAD_DIM] * jnp.exp(ac)
            y_scr[:, hs] = yd + yo
            a_last = a_cs[L - 1:L, h:h + 1]
            xs_scr[:, hs] = xh * (jnp.exp(a_last - ac) * dt[:, h:h + 1])
        sts = _dot(xs_scr[:, g * gw:(g + 1) * gw].T.astype(BF16), bg)
        for hh in range(hg):
            h = g * hg + hh
            rs = slice(h * SSM_HEAD_DIM, (h + 1) * SSM_HEAD_DIM)
            a_last = a_cs[L - 1:L, h:h + 1]
            state_scr[rs, :] = (jnp.exp(a_last) * state_scr[rs, :]
                                + sts[hh * SSM_HEAD_DIM:(hh + 1) * SSM_HEAD_DIM, :])

    gated = (y_scr[...] + dsk_ref[...] * x) * _silu(zb_ref[...])
    ms = jnp.mean(gated * gated, axis=-1, keepdims=True)
    ob_ref[...] = (gated * lax.rsqrt(ms + EPS) * gs_ref[...]).astype(ob_ref.dtype)

    @pl.when(c == nc - 1)
    def _():
        hfin_ref[0] = state_scr[...]


def _ssd(xd_arr, xd_colblk, zb_arr, zb_colblk, prev8, h0, conv_w, conv_b, dt_bias, a_log, d_skip, g_ssm,
         bsz, nc, L, t_valid):
    rows = bsz * nc * L
    pad16 = lambda a: jnp.pad(a, (0, LANES - SSM_HEADS)).reshape(1, LANES)
    const = lambda shape: pl.BlockSpec(shape, lambda b, c: (0,) * len(shape))
    return pl.pallas_call(
        functools.partial(_ssd_kernel, L=L, t_valid=t_valid, nc=nc),
        out_shape=(jax.ShapeDtypeStruct((rows, W_BRANCH), BF16),
                   jax.ShapeDtypeStruct((bsz, SSM_HEADS * SSM_HEAD_DIM, SSM_STATE), F32)),
        grid=(bsz, nc),
        in_specs=[pl.BlockSpec((L, XD_WIDTH), lambda b, c: (b * nc + c, xd_colblk)),
                  pl.BlockSpec((L, W_BRANCH), lambda b, c: (b * nc + c, zb_colblk)),
                  pl.BlockSpec((1, SUBLANES, CONV_DIM), lambda b, c: (b, 0, 0)),
                  pl.BlockSpec((1, SSM_HEADS * SSM_HEAD_DIM, SSM_STATE), lambda b, c: (b, 0, 0)),
                  const((SSM_CONV, CONV_DIM)), const((1, CONV_DIM)), const((1, LANES)), const((1, LANES)),
                  const((1, W_BRANCH)), const((1, W_BRANCH))],
        out_specs=(pl.BlockSpec((L, W_BRANCH), lambda b, c: (b * nc + c, 0)),
                   pl.BlockSpec((1, SSM_HEADS * SSM_HEAD_DIM, SSM_STATE), lambda b, c: (b, 0, 0))),
        scratch_shapes=[pltpu.VMEM((SSM_HEADS * SSM_HEAD_DIM, SSM_STATE), F32),
                        pltpu.VMEM((SUBLANES, CONV_DIM), F32),
                        pltpu.VMEM((L, W_BRANCH), F32),
                        pltpu.VMEM((L, W_BRANCH), F32)],
        compiler_params=_params("arbitrary", "arbitrary"),
        name="ssd",
    )(xd_arr, zb_arr, prev8, h0, conv_w, conv_b.reshape(1, -1), pad16(dt_bias), pad16(a_log),
      jnp.repeat(d_skip, SSM_HEAD_DIM).reshape(1, -1), g_ssm.reshape(1, -1))


def _select_topk(gate, n_elig, axis):
    idx = lax.broadcasted_iota(jnp.int32, gate.shape, axis).astype(F32)
    big = jnp.float32(1 << 30)
    g = jnp.where(idx < n_elig.astype(F32), gate, -jnp.inf)
    sel = jnp.zeros(gate.shape, jnp.bool_)
    for _ in range(MOBA_TOPK):
        m = jnp.max(g, axis=axis, keepdims=True)
        hit = (g == m) & (m > -jnp.inf) & (m < jnp.inf)
        first = jnp.min(jnp.where(hit, idx, big), axis=axis, keepdims=True)
        pick = idx == first
        sel = sel | pick
        g = jnp.where(pick, -jnp.inf, g)
    return sel


def _attn_kernel(q_ref, k_ref, v_ref, zc_ref, sl_ref, o_ref, kt_ref, vt_ref, kbf, vtb, km_scr, *, T, tq):
    nb = T // MOBA_BLOCK
    nbp = km_scr.shape[0]
    qb = pl.program_id(2)

    @pl.when(qb == 0)
    def _():
        km_scr[...] = jnp.zeros(km_scr.shape, F32)
        for n in range(nb):
            rs = slice(n * MOBA_BLOCK, (n + 1) * MOBA_BLOCK)
            k = k_ref[rs, :]
            vt = v_ref[rs, :].T
            kt_ref[0, :, rs] = k.T
            vt_ref[0, :, rs] = vt
            kbf[n] = k.astype(BF16)
            vtb[n] = vt.astype(BF16)
            km_scr[n:n + 1, :] = jnp.mean(k, axis=0, keepdims=True)

    q = q_ref[...]
    lane = lax.broadcasted_iota(jnp.int32, (tq, LANES), 1)
    krow = lax.broadcasted_iota(jnp.int32, (MOBA_BLOCK, tq), 0)
    qcol = lax.broadcasted_iota(jnp.int32, (MOBA_BLOCK, tq), 1)
    rel = (krow - qcol).astype(F32)
    brow = lax.broadcasted_iota(jnp.int32, (nbp, tq), 0)
    scale = ATT_HEAD_DIM ** -0.5
    k1, k2, _ = _split3(km_scr[...])
    nh = LANES // ATT_HEAD_DIM
    qss, selbs, slopes, alibis = [], [], [], []
    for i in range(nh):
        qh = jnp.where(lane // ATT_HEAD_DIM == i, q, 0.0)
        q1, q2, _ = _split3(qh)
        gate = _dot_nt(k1, q1) + _dot_nt(k1, q2) + _dot_nt(k2, q1)
        selbs.append(jnp.where(_select_topk(gate, qb, 0), 0.0, NEG))
        qss.append((qh * scale).astype(BF16))
        slopes.append(sl_ref[0, :, i * ATT_HEAD_DIM:i * ATT_HEAD_DIM + 1])
        alibis.append(slopes[i] * rel)
    qs_all = jnp.concatenate(qss, axis=0)

    def logits(n, extra):
        s_all = _dot_nt(kbf[n], qs_all)
        out = []
        for i in range(nh):
            rowterm = (jnp.sum(jnp.where(brow == n, selbs[i], 0.0), axis=0, keepdims=True)
                       + slopes[i] * ((n - qb) * MOBA_BLOCK).astype(F32) + extra)
            out.append(s_all[:, i * tq:(i + 1) * tq] + alibis[i] + rowterm)
        return out

    def update(i, blocks, carry):
        m, l, acc = carry
        m_new = m
        for _, s in blocks:
            m_new = jnp.maximum(m_new, jnp.max(s, axis=0, keepdims=True))
        alpha = jnp.exp(m - m_new)
        l = alpha * l
        acc = alpha * acc
        for n, s in blocks:
            p = jnp.exp(s - m_new)
            l = l + jnp.sum(p, axis=0, keepdims=True)
            acc = acc + _dot(vtb[n, pl.ds(i * ATT_HEAD_DIM, ATT_HEAD_DIM), :], p.astype(BF16))
        return m_new, l, acc

    def body(j, carry):
        n0 = 2 * j
        sa, sb = logits(n0, 0.0), logits(n0 + 1, 0.0)
        return tuple(update(i, [(n0, sa[i]), (n0 + 1, sb[i])], carry[i]) for i in range(nh))

    init = (jnp.full((1, tq), NEG, F32), jnp.zeros((1, tq), F32), jnp.zeros((ATT_HEAD_DIM, tq), F32))
    carry = lax.fori_loop(0, qb // 2, body, (init,) * nh)
    left = jnp.maximum(qb - 1, 0)
    s_left = logits(left, jnp.where(qb % 2 == 1, 0.0, NEG))
    s_qk = _dot_nt(kbf[qb], qs_all)
    outs = []
    for i in range(nh):
        s_own = jnp.where(krow <= qcol, s_qk[:, i * tq:(i + 1) * tq] + alibis[i], NEG)
        m, l, acc = update(i, [(left, s_left[i]), (qb, s_own)], carry[i])
        outs.append(acc / l)
    o = jnp.concatenate(outs, axis=0).T
    o_ref[...] = (o * _silu(zc_ref[...])).astype(o_ref.dtype)


def _alibi_slopes():
    return 2.0 ** (-8.0 * jnp.arange(1, ATT_HEADS + 1, dtype=F32) / ATT_HEADS)


def _attn_prompt(proj, bsz, T):
    tq = MOBA_BLOCK
    nq = T // tq
    nb = T // MOBA_BLOCK
    nbp = -(-nb // SUBLANES) * SUBLANES
    hp_n = W_BRANCH // LANES
    slopes = jnp.repeat(_alibi_slopes(), ATT_HEAD_DIM).reshape(hp_n, 1, LANES)
    cpb = W_BRANCH // LANES
    kv_t = lambda: pl.BlockSpec((1, LANES, T), lambda b, hp, qi: (b, hp, 0))
    return pl.pallas_call(
        functools.partial(_attn_kernel, T=T, tq=tq),
        out_shape=(jax.ShapeDtypeStruct((bsz * T, W_BRANCH), BF16),
                   jax.ShapeDtypeStruct((bsz, W_BRANCH, T), F32),
                   jax.ShapeDtypeStruct((bsz, W_BRANCH, T), F32)),
        grid=(bsz, hp_n, nq),
        in_specs=[pl.BlockSpec((tq, LANES), lambda b, hp, qi: (b * nq + qi, COL_Q * cpb + hp)),
                  pl.BlockSpec((T, LANES), lambda b, hp, qi: (b, COL_K * cpb + hp)),
                  pl.BlockSpec((T, LANES), lambda b, hp, qi: (b, COL_VA * cpb + hp)),
                  pl.BlockSpec((tq, LANES), lambda b, hp, qi: (b * nq + qi, COL_ZC * cpb + hp)),
                  pl.BlockSpec((1, 1, LANES), lambda b, hp, qi: (hp, 0, 0))],
        out_specs=(pl.BlockSpec((tq, LANES), lambda b, hp, qi: (b * nq + qi, hp)), kv_t(), kv_t()),
        scratch_shapes=[pltpu.VMEM((nb, MOBA_BLOCK, LANES), BF16), pltpu.VMEM((nb, LANES, MOBA_BLOCK), BF16),
                        pltpu.VMEM((nbp, LANES), F32)],
        compiler_params=_params("arbitrary", "arbitrary", "arbitrary"),
        name="attn_prompt",
    )(proj, proj, proj, proj, slopes)


PAGES_PER_STEP = 16
BLOCK_PAGES = MOBA_BLOCK // PAGE_SIZE


def _kmean_kernel(pt_ref, *refs):
    o_ref = refs[PAGES_PER_STEP]
    for i in range(PAGES_PER_STEP // BLOCK_PAGES):
        s = refs[i * BLOCK_PAGES][0]
        for j in range(1, BLOCK_PAGES):
            s = s + refs[i * BLOCK_PAGES + j][0]
        o_ref[0, i] = jnp.sum(s, axis=-1) * (1.0 / MOBA_BLOCK)


def _kmean_paged(cache_t, page_off, pt_flat, bsz, n_pages):
    steps = n_pages // PAGES_PER_STEP
    bps = PAGES_PER_STEP // BLOCK_PAGES

    def page_spec(i):
        return pl.BlockSpec((1, ATT_HEADS, ATT_HEAD_DIM, PAGE_SIZE),
                            lambda b, s, pt: (page_off + pt[b * n_pages + s * PAGES_PER_STEP + i], 0, 0, 0))

    return pl.pallas_call(
        _kmean_kernel,
        out_shape=jax.ShapeDtypeStruct((bsz, n_pages // BLOCK_PAGES, ATT_HEADS, ATT_HEAD_DIM), F32),
        grid_spec=pltpu.PrefetchScalarGridSpec(
            num_scalar_prefetch=1,
            grid=(bsz, steps),
            in_specs=[page_spec(i) for i in range(PAGES_PER_STEP)],
            out_specs=pl.BlockSpec((1, bps, ATT_HEADS, ATT_HEAD_DIM), lambda b, s, pt: (b, s, 0, 0))),
        compiler_params=_params("arbitrary", "arbitrary"),
        name="kmean_paged",
    )(pt_flat, *([cache_t] * PAGES_PER_STEP))


def _gate_topk_kernel(q_ref, km_ref, o_ref):
    gate = jnp.sum(km_ref[0] * q_ref[...], axis=-1)
    idx = lax.broadcasted_iota(jnp.int32, gate.shape, 0).astype(F32)
    rid = lax.broadcasted_iota(jnp.int32, (SUBLANES, ATT_HEADS), 0)
    big = jnp.float32(1 << 30)
    out = jnp.zeros((SUBLANES, ATT_HEADS), F32)
    g = gate
    for r in range(MOBA_TOPK):
        m = jnp.max(g, axis=0, keepdims=True)
        first = jnp.min(jnp.where(g == m, idx, big), axis=0, keepdims=True)
        out = jnp.where(rid == r, jnp.broadcast_to(first, (SUBLANES, ATT_HEADS)), out)
        g = jnp.where(idx == first, -jnp.inf, g)
    o_ref[0] = out.astype(jnp.int32)


def _gate_topk(q3, kmean):
    bsz, nblk = kmean.shape[:2]
    return pl.pallas_call(
        _gate_topk_kernel,
        out_shape=jax.ShapeDtypeStruct((bsz, SUBLANES, ATT_HEADS), jnp.int32),
        grid=(bsz,),
        in_specs=[pl.BlockSpec((1, ATT_HEADS, ATT_HEAD_DIM), lambda b: (b, 0, 0)),
                  pl.BlockSpec((1, nblk, ATT_HEADS, ATT_HEAD_DIM), lambda b: (b, 0, 0, 0))],
        out_specs=pl.BlockSpec((1, SUBLANES, ATT_HEADS), lambda b: (b, 0, 0)),
        compiler_params=_params("arbitrary"),
        name="gate_topk",
    )(q3, kmean)


N_SEL_TILES = MOBA_TOPK * BLOCK_PAGES


def _attn_sample_kernel(pt_ref, idx_ref, q_ref, kn_ref, vn_ref, zc_ref, sl_ref, *refs, past_len):
    k_refs = refs[:N_SEL_TILES]
    v_refs = refs[N_SEL_TILES:2 * N_SEL_TILES]
    o_ref = refs[2 * N_SEL_TILES]
    b = pl.program_id(0)
    h = pl.program_id(1)
    scale = ATT_HEAD_DIM ** -0.5
    q = jnp.broadcast_to(q_ref[0, 0] * scale, (SUBLANES, ATT_HEAD_DIM)).astype(BF16)
    slope = sl_ref[0]
    lane = lax.broadcasted_iota(jnp.int32, (SUBLANES, PAGE_SIZE), 1)
    logits = []
    for j in range(MOBA_TOPK):
        blk = idx_ref[(b * MOBA_TOPK + j) * ATT_HEADS + h]
        for half in range(BLOCK_PAGES):
            kpos = blk * MOBA_BLOCK + half * PAGE_SIZE + lane
            s = _dot(q, k_refs[j * BLOCK_PAGES + half][0, 0].astype(BF16))
            logits.append(s - slope * (past_len - kpos).astype(F32))
    bf = lambda a: a.astype(BF16).astype(F32)
    s_own = jnp.sum(q.astype(F32) * bf(kn_ref[0, 0]), axis=1, keepdims=True)
    m = s_own
    for s in logits:
        m = jnp.maximum(m, jnp.max(s, axis=1, keepdims=True))
    p_own = jnp.exp(s_own - m)
    l = p_own
    acc = bf(p_own) * bf(vn_ref[0, 0])
    for t, s in enumerate(logits):
        p = jnp.exp(s - m)
        l = l + jnp.sum(p, axis=1, keepdims=True)
        acc = acc + _dot_nt(p.astype(BF16), v_refs[t][0, 0].astype(BF16))
    o = (acc / l) * _silu(zc_ref[0, 0])
    o_ref[0, 0] = o[0:1, :]


def _attn_sample(cache_kt, cache_vt, page_off, pt_flat, idx_flat, q4, kn4, vn4, zc4, n_pages):
    bsz = q4.shape[0]
    past_len = n_pages * PAGE_SIZE
    slopes = jnp.broadcast_to(_alibi_slopes()[:, None, None], (ATT_HEADS, 1, PAGE_SIZE))

    def tile_spec(j, half):
        def imap(b, h, pt, idx):
            blk = idx[(b * MOBA_TOPK + j) * ATT_HEADS + h]
            return (page_off + pt[b * n_pages + blk * BLOCK_PAGES + half], h, 0, 0)
        return pl.BlockSpec((1, 1, ATT_HEAD_DIM, PAGE_SIZE), imap)

    tiles = [tile_spec(j, half) for j in range(MOBA_TOPK) for half in range(BLOCK_PAGES)]
    head = lambda: pl.BlockSpec((1, 1, 1, ATT_HEAD_DIM), lambda b, h, pt, idx: (b, h, 0, 0))
    return pl.pallas_call(
        functools.partial(_attn_sample_kernel, past_len=past_len),
        out_shape=jax.ShapeDtypeStruct((bsz, ATT_HEADS, 1, ATT_HEAD_DIM), F32),
        grid_spec=pltpu.PrefetchScalarGridSpec(
            num_scalar_prefetch=2,
            grid=(bsz, ATT_HEADS),
            in_specs=[head(), head(), head(), head(),
                      pl.BlockSpec((1, 1, PAGE_SIZE), lambda b, h, pt, idx: (h, 0, 0))] + tiles + tiles,
            out_specs=head()),
        compiler_params=_params("arbitrary", "arbitrary"),
        name="attn_sample",
    )(pt_flat, idx_flat, q4, kn4, vn4, zc4, slopes,
      *([cache_kt] * N_SEL_TILES), *([cache_vt] * N_SEL_TILES))


def _merge_kernel(oa_ref, ob_ref, oc_ref, ga_ref, gb_ref, gc_ref, w_ref, o_ref):
    acc = _sigmoid(ga_ref[...]) * _dot(oa_ref[...], w_ref[0])
    acc = acc + _sigmoid(gb_ref[...]) * _dot(ob_ref[...], w_ref[1])
    acc = acc + _sigmoid(gc_ref[...]) * _dot(oc_ref[...], w_ref[2])
    o_ref[...] = acc.astype(o_ref.dtype)


def _merge(o_a, o_b, o_c, proj, w_branch, tm, tn):
    m = o_a.shape[0]
    nt = D_MODEL // tn
    gate_blk0 = GATE_COLBLK * D_MODEL // tn
    br = lambda: pl.BlockSpec((tm, W_BRANCH), lambda j, i: (i, 0))
    gate = lambda r: pl.BlockSpec((tm, tn), lambda j, i: (i, gate_blk0 + r * nt + j))
    return pl.pallas_call(
        _merge_kernel,
        out_shape=jax.ShapeDtypeStruct((m, D_MODEL), BF16),
        grid=(nt, m // tm),
        in_specs=[br(), br(), br(), gate(0), gate(1), gate(2),
                  pl.BlockSpec((N_BRANCH, W_BRANCH, tn), lambda j, i: (0, 0, j))],
        out_specs=pl.BlockSpec((tm, tn), lambda j, i: (i, j)),
        compiler_params=_params("parallel", "parallel"),
        name="merge",
    )(o_a, o_b, o_c, proj, proj, proj, w_branch)


def _outproj_kernel(m_ref, w_ref, x_ref, g_ref, o_ref):
    out = _dot(m_ref[...], w_ref[...])
    ms = jnp.mean(out * out, axis=-1, keepdims=True)
    o_ref[...] = x_ref[...] + out * lax.rsqrt(ms + EPS) * g_ref[...]


def _outproj(merged, w_out, x, g_post, tm):
    m = merged.shape[0]
    return pl.pallas_call(
        _outproj_kernel,
        out_shape=jax.ShapeDtypeStruct((m, D_MODEL), F32),
        grid=(m // tm,),
        in_specs=[pl.BlockSpec((tm, D_MODEL), lambda i: (i, 0)),
                  pl.BlockSpec((D_MODEL, D_MODEL), lambda i: (0, 0)),
                  pl.BlockSpec((tm, D_MODEL), lambda i: (i, 0)),
                  pl.BlockSpec((1, D_MODEL), lambda i: (0, 0))],
        out_specs=pl.BlockSpec((tm, D_MODEL), lambda i: (i, 0)),
        compiler_params=_params("parallel"),
        name="outproj",
    )(merged, w_out, x, g_post.reshape(1, -1))


def _reorder_w_in(w):
    u, v, za, xbc, zb, dt, q, k, va, zc, gl = jnp.split(w.T, [int(i) for i in np.cumsum(IN_SIZES)[:-1]], axis=0)
    pad = jnp.zeros((XD_WIDTH - CONV_DIM - SSM_HEADS, w.shape[0]), w.dtype)
    return jnp.concatenate([u, v, za, zb, q, k, va, zc, gl, xbc, dt, pad], axis=0).astype(BF16)


def _seg(proj, blk, rows=None):
    out = proj[:, blk * W_BRANCH:(blk + 1) * W_BRANCH]
    return out if rows is None else out[:rows]


def _prompt_layer(x, bsz, T, lw):
    (g_pre, w_in, w_s, b_s, ln_g, ln_b, conv_w, conv_b, dt_bias, a_log, d_skip, g_ssm, w_br, w_out, g_post) = lw
    m = bsz * T
    h = _prenorm(x, g_pre, 256)
    proj = _inproj(h, w_in, 512, 1024)
    o_a = _gmlp(proj, w_s, b_s, ln_g, ln_b, 512)
    L = min(SSM_CHUNK, T)
    nc = T // L
    prev8 = jnp.zeros((bsz, SUBLANES, CONV_DIM), F32)
    h0 = jnp.zeros((bsz, SSM_HEADS * SSM_HEAD_DIM, SSM_STATE), F32)
    o_b, h_fin = _ssd(proj, XD_COLBLK, proj, COL_ZB, prev8, h0, conv_w, conv_b, dt_bias, a_log, d_skip, g_ssm,
                      bsz, nc, L, T)
    o_c, k_t, v_t = _attn_prompt(proj, bsz, T)
    merged = _merge(o_a, o_b, o_c, proj, w_br, 512, 1024)
    y = _outproj(merged, w_out, x, g_post, 256)
    k_new = k_t.reshape(bsz, ATT_HEADS, ATT_HEAD_DIM, T).transpose(0, 3, 1, 2)
    v_new = v_t.reshape(bsz, ATT_HEADS, ATT_HEAD_DIM, T).transpose(0, 3, 1, 2)
    xbc_raw = proj[:, XD_COLBLK * XD_WIDTH:XD_COLBLK * XD_WIDTH + CONV_DIM].reshape(bsz, T, CONV_DIM)
    conv_new = xbc_raw[:, T - (SSM_CONV - 1):]
    ssm_new = h_fin.reshape(bsz, SSM_HEADS, SSM_HEAD_DIM, SSM_STATE)
    return y, k_new, v_new, ssm_new, conv_new


SAMPLE_ROWS = 16
SAMPLE_L = 128


def _sample_layer(x16, bsz, conv_prev, ssm_prev, cache_kt, cache_vt, page_off, pt_flat, n_pages, lw):
    (g_pre, w_in, w_s, b_s, ln_g, ln_b, conv_w, conv_b, dt_bias, a_log, d_skip, g_ssm, w_br, w_out, g_post) = lw
    h = _prenorm(x16, g_pre, SAMPLE_ROWS)
    proj = _inproj(h, w_in, SAMPLE_ROWS, 1024)
    o_a, v_a = _gmlp_single(proj, w_s, b_s, ln_g, ln_b)
    xd = proj[:bsz, XD_COLBLK * XD_WIDTH:]
    xd_rows = jnp.zeros((bsz, SAMPLE_L, XD_WIDTH), F32).at[:, 0].set(xd).reshape(bsz * SAMPLE_L, XD_WIDTH)
    zb_rows = jnp.zeros((bsz, SAMPLE_L, W_BRANCH), F32).at[:, 0].set(_seg(proj, COL_ZB, bsz))
    zb_rows = zb_rows.reshape(bsz * SAMPLE_L, W_BRANCH)
    prev8 = jnp.concatenate([jnp.zeros((bsz, SUBLANES - (SSM_CONV - 1), CONV_DIM), F32), conv_prev], axis=1)
    h0 = ssm_prev.reshape(bsz, SSM_HEADS * SSM_HEAD_DIM, SSM_STATE)
    ob_rows, h_fin = _ssd(xd_rows, 0, zb_rows, 0, prev8, h0, conv_w, conv_b, dt_bias, a_log, d_skip, g_ssm,
                          bsz, 1, SAMPLE_L, 1)
    o_b = ob_rows.reshape(bsz, SAMPLE_L, W_BRANCH)[:, 0]
    o_b = jnp.pad(o_b, ((0, SAMPLE_ROWS - bsz), (0, 0)))
    conv_new = jnp.concatenate([conv_prev[:, 1:], xd[:, None, :CONV_DIM]], axis=1)
    ssm_new = h_fin.reshape(bsz, SSM_HEADS, SSM_HEAD_DIM, SSM_STATE)
    q = _seg(proj, COL_Q, bsz)
    k_new = _seg(proj, COL_K, bsz)
    v_new = _seg(proj, COL_VA, bsz)
    kmean = _kmean_paged(cache_kt, page_off, pt_flat, bsz, n_pages)
    top = _gate_topk(q.reshape(bsz, ATT_HEADS, ATT_HEAD_DIM), kmean)
    idx_flat = top[:, :MOBA_TOPK].reshape(-1)
    heads4 = lambda a: a.reshape(bsz, ATT_HEADS, 1, ATT_HEAD_DIM)
    att = _attn_sample(cache_kt, cache_vt, page_off, pt_flat, idx_flat, heads4(q), heads4(k_new), heads4(v_new),
                       heads4(_seg(proj, COL_ZC, bsz)), n_pages)
    o_c = jnp.pad(att.reshape(bsz, W_BRANCH), ((0, SAMPLE_ROWS - bsz), (0, 0))).astype(BF16)
    merged = _merge(o_a, o_b, o_c, proj, w_br, SAMPLE_ROWS, 1024)
    y = _outproj(merged, w_out, x16, g_post, SAMPLE_ROWS)
    shape4 = (bsz, 1, ATT_HEADS, ATT_HEAD_DIM)
    return (y, k_new.reshape(shape4), v_new.reshape(shape4), ssm_new, conv_new,
            v_a[:bsz].reshape(bsz, 1, W_BRANCH))


def kernel(x_prompt, x_sample, cache_k, cache_v, state_ssm, state_conv, page_table, g_pre, w_in, w_spatial,
           b_spatial, ln_v_g, ln_v_b, conv_w, conv_b, dt_bias, a_log, d_skip, g_ssm, w_branch, w_out, g_post):
    depth = w_in.shape[0]
    bp, T, _ = x_prompt.shape
    bs = x_sample.shape[0]
    n_phys = cache_k.shape[1]
    n_pages = page_table.shape[1]
    pt_flat = page_table.reshape(-1).astype(jnp.int32)
    xp = x_prompt.reshape(bp * T, D_MODEL)
    xs = jnp.pad(x_sample.reshape(bs, D_MODEL), ((0, SAMPLE_ROWS - bs), (0, 0)))
    to_t = lambda c: c.transpose(0, 1, 3, 4, 2).reshape(depth * n_phys, ATT_HEADS, ATT_HEAD_DIM, PAGE_SIZE)
    cache_kt, cache_vt = to_t(cache_k), to_t(cache_v)
    outs = [[] for _ in range(9)]
    for l in range(depth):
        lw = (g_pre[l], _reorder_w_in(w_in[l]), w_spatial[l], b_spatial[l], ln_v_g[l], ln_v_b[l], conv_w[l],
              conv_b[l], dt_bias[l], a_log[l], d_skip[l], g_ssm[l], w_branch[l].astype(BF16),
              w_out[l].astype(BF16), g_post[l])
        xp, kp, vp, sp, cp = _prompt_layer(xp, bp, T, lw)
        xs, ks, vs, ss, cs, gv = _sample_layer(xs, bs, state_conv[l], state_ssm[l], cache_kt, cache_vt,
                                               l * n_phys, pt_flat, n_pages, lw)
        for lst, val in zip(outs, (kp, vp, sp, cp, ks, vs, ss, cs, gv)):
            lst.append(val)
    return (xp.reshape(bp, T, D_MODEL), xs[:bs].reshape(bs, 1, D_MODEL)) + tuple(jnp.stack(o) for o in outs)
```

```python
import functools

import jax
import jax.numpy as jnp
import numpy as np
from jax import lax
from jax.experimental import pallas as pl
from jax.experimental.pallas import tpu as pltpu

F32 = jnp.float32
BF16 = jnp.bfloat16

D_MODEL = 2048
W_BRANCH = 1024
N_BRANCH = 3
GMLP_GROUPS = 4
GMLP_CHUNK = 128
GMLP_GW = W_BRANCH // GMLP_GROUPS
SSM_HEAD_DIM = 64
SSM_HEADS = 16
SSM_GROUPS = 2
SSM_STATE = 128
SSM_CONV = 4
SSM_CHUNK = 256
CONV_DIM = W_BRANCH + 2 * SSM_GROUPS * SSM_STATE
ATT_HEAD_DIM = 64
ATT_HEADS = 16
MOBA_BLOCK = 256
MOBA_TOPK = 3
PAGE_SIZE = 128
EPS = 1e-6
IN_SIZES = [W_BRANCH, W_BRANCH, W_BRANCH, CONV_DIM, W_BRANCH, SSM_HEADS,
            W_BRANCH, W_BRANCH, W_BRANCH, W_BRANCH, N_BRANCH * D_MODEL]

LANES = 128
SUBLANES = 8
VMEM_LIMIT = 56 * 1024 * 1024

COL_U, COL_V, COL_ZA, COL_ZB, COL_Q, COL_K, COL_VA, COL_ZC = range(8)
GATE_COLBLK = 4
XD_COLBLK = 7
XD_WIDTH = D_MODEL
N_PROJ = 8 * W_BRANCH + N_BRANCH * D_MODEL + XD_WIDTH
NEG = -1e30


def _params(*sem):
    return pltpu.CompilerParams(dimension_semantics=sem, vmem_limit_bytes=VMEM_LIMIT)


def _sigmoid(x):
    return 1.0 / (1.0 + jnp.exp(-x))


def _silu(x):
    return x * _sigmoid(x)


def _gelu(x):
    return 0.5 * x * (1.0 + jnp.tanh(0.7978845608028654 * (x + 0.044715 * (x * x * x))))


def _softplus(x):
    return jnp.maximum(x, 0.0) + jnp.log1p(jnp.exp(-jnp.abs(x)))


def _dot(a, b):
    return jnp.dot(a, b, preferred_element_type=F32)


def _dot_nt(a, b):
    return lax.dot_general(a, b, (((1,), (1,)), ((), ())), preferred_element_type=F32)


def _split3(x):
    x1 = x.astype(BF16)
    r1 = x - x1.astype(F32)
    x2 = r1.astype(BF16)
    r2 = r1 - x2.astype(F32)
    return x1, x2, r2.astype(BF16)


def _prenorm_kernel(x_ref, g_ref, o_ref):
    x = x_ref[...]
    ms = jnp.mean(x * x, axis=-1, keepdims=True)
    o_ref[...] = (x * lax.rsqrt(ms + EPS) * g_ref[...]).astype(o_ref.dtype)


def _prenorm(x, g, tm):
    m, d = x.shape
    return pl.pallas_call(
        _prenorm_kernel,
        out_shape=jax.ShapeDtypeStruct((m, d), BF16),
        grid=(m // tm,),
        in_specs=[pl.BlockSpec((tm, d), lambda i: (i, 0)),
                  pl.BlockSpec((1, d), lambda i: (0, 0))],
        out_specs=pl.BlockSpec((tm, d), lambda i: (i, 0)),
        compiler_params=_params("parallel"),
        name="prenorm",
    )(x, g.reshape(1, d))


def _mm_kernel(x_ref, w_ref, o_ref):
    o_ref[...] = _dot_nt(x_ref[...], w_ref[...])


def _inproj(h, w_t, tm, tn):
    m, k = h.shape
    n = w_t.shape[0]
    return pl.pallas_call(
        _mm_kernel,
        out_shape=jax.ShapeDtypeStruct((m, n), F32),
        grid=(n // tn, m // tm),
        in_specs=[pl.BlockSpec((tm, k), lambda j, i: (i, 0)),
                  pl.BlockSpec((tn, k), lambda j, i: (j, 0))],
        out_specs=pl.BlockSpec((tm, tn), lambda j, i: (i, j)),
        compiler_params=_params("parallel", "parallel"),
        name="inproj",
    )(h, w_t)


def _layernorm_rows(x, g, b):
    mu = jnp.mean(x, axis=-1, keepdims=True)
    d = x - mu
    var = jnp.mean(d * d, axis=-1, keepdims=True)
    return d * lax.rsqrt(var + EPS) * g + b


def _gmlp_kernel(u_ref, v_ref, z_ref, ws_ref, bt_ref, lg_ref, lb_ref, o_ref, *, cc, nch):
    row = lax.broadcasted_iota(jnp.int32, (cc, cc), 0)
    col = lax.broadcasted_iota(jnp.int32, (cc, cc), 1)
    tri = row >= col
    for c in range(nch):
        sl = pl.ds(c * cc, cc)
        va = _layernorm_rows(_gelu(v_ref[sl, :]), lg_ref[...], lb_ref[...])
        for g in range(GMLP_GROUPS):
            gs = slice(g * GMLP_GW, (g + 1) * GMLP_GW)
            wc = jnp.where(tri, ws_ref[g], 0.0).astype(BF16)
            mixed = _dot(wc, va[:, gs].astype(BF16)) + bt_ref[:, g:g + 1]
            o = _gelu(u_ref[sl, gs]) * mixed * _silu(z_ref[sl, gs])
            o_ref[sl, gs] = o.astype(o_ref.dtype)


def _gmlp(proj, w_s, b_s, ln_g, ln_b, tr):
    m = proj.shape[0]
    cc = GMLP_CHUNK
    return pl.pallas_call(
        functools.partial(_gmlp_kernel, cc=cc, nch=tr // cc),
        out_shape=jax.ShapeDtypeStruct((m, W_BRANCH), BF16),
        grid=(m // tr,),
        in_specs=[pl.BlockSpec((tr, W_BRANCH), lambda i: (i, COL_U)),
                  pl.BlockSpec((tr, W_BRANCH), lambda i: (i, COL_V)),
                  pl.BlockSpec((tr, W_BRANCH), lambda i: (i, COL_ZA)),
                  pl.BlockSpec((GMLP_GROUPS, cc, cc), lambda i: (0, 0, 0)),
                  pl.BlockSpec((cc, GMLP_GROUPS), lambda i: (0, 0)),
                  pl.BlockSpec((1, W_BRANCH), lambda i: (0, 0)),
                  pl.BlockSpec((1, W_BRANCH), lambda i: (0, 0))],
        out_specs=pl.BlockSpec((tr, W_BRANCH), lambda i: (i, 0)),
        compiler_params=_params("parallel"),
        name="gmlp",
    )(proj, proj, proj, w_s, b_s.T, ln_g.reshape(1, -1), ln_b.reshape(1, -1))


def _gmlp1_kernel(u_ref, v_ref, z_ref, w0_ref, b0_ref, lg_ref, lb_ref, o_ref, va_ref):
    va = _layernorm_rows(_gelu(v_ref[...]), lg_ref[...], lb_ref[...])
    va_ref[...] = va
    mixed = w0_ref[...] * va + b0_ref[...]
    o_ref[...] = (_gelu(u_ref[...]) * mixed * _silu(z_ref[...])).astype(o_ref.dtype)


def _gmlp_single(proj, w_s, b_s, ln_g, ln_b):
    m = proj.shape[0]
    w0 = jnp.repeat(w_s[:, 0, 0], GMLP_GW).reshape(1, W_BRANCH)
    b0 = jnp.repeat(b_s[:, 0], GMLP_GW).reshape(1, W_BRANCH)
    row = lambda: pl.BlockSpec((1, W_BRANCH), lambda i: (0, 0))
    return pl.pallas_call(
        _gmlp1_kernel,
        out_shape=(jax.ShapeDtypeStruct((m, W_BRANCH), BF16),
                   jax.ShapeDtypeStruct((m, W_BRANCH), F32)),
        grid=(1,),
        in_specs=[pl.BlockSpec((m, W_BRANCH), lambda i: (0, COL_U)),
                  pl.BlockSpec((m, W_BRANCH), lambda i: (0, COL_V)),
                  pl.BlockSpec((m, W_BRANCH), lambda i: (0, COL_ZA)),
                  row(), row(), row(), row()],
        out_specs=(pl.BlockSpec((m, W_BRANCH), lambda i: (0, 0)),
                   pl.BlockSpec((m, W_BRANCH), lambda i: (0, 0))),
        compiler_params=_params("arbitrary"),
        name="gmlp_single",
    )(proj, proj, proj, w0, b0, ln_g.reshape(1, -1), ln_b.reshape(1, -1))


def _ssd_kernel(xd_ref, zb_ref, prev_ref, h0_ref, cw_ref, cb_ref, dtb_ref, alog_ref, dsk_ref, gs_ref,
                ob_ref, hfin_ref, state_scr, tail_scr, y_scr, xs_scr, *, L, t_valid, nc):
    c = pl.program_id(1)

    @pl.when(c == 0)
    def _():
        state_scr[...] = h0_ref[0]
        tail_scr[...] = prev_ref[0]

    raw = xd_ref[:, :CONV_DIM]
    tail = tail_scr[...]
    rid = lax.broadcasted_iota(jnp.int32, (SUBLANES, CONV_DIM), 0)
    acc = cb_ref[...] + raw * cw_ref[SSM_CONV - 1:SSM_CONV, :]
    for k in range(1, SSM_CONV):
        rolled = pltpu.roll(raw, k, axis=0)
        head = jnp.where(rid < k, pltpu.roll(tail, k, axis=0), rolled[:SUBLANES])
        shifted = jnp.concatenate([head, rolled[SUBLANES:]], axis=0)
        acc = acc + shifted * cw_ref[SSM_CONV - 1 - k:SSM_CONV - k, :]
    tail_scr[...] = raw[L - SUBLANES:, :]
    xbc = _silu(acc)
    x = xbc[:, :W_BRANCH]
    gn = SSM_GROUPS * SSM_STATE
    bm = xbc[:, W_BRANCH:W_BRANCH + gn].astype(BF16)
    cm = xbc[:, W_BRANCH + gn:].astype(BF16)

    dt = _softplus(xd_ref[:, CONV_DIM:CONV_DIM + LANES] + dtb_ref[...])
    if t_valid < nc * L:
        grow = c * L + lax.broadcasted_iota(jnp.int32, (L, LANES), 0)
        dt = jnp.where(grow < t_valid, dt, 0.0)
    da = dt * (-jnp.exp(alog_ref[...]))
    row = lax.broadcasted_iota(jnp.int32, (L, L), 0)
    col = lax.broadcasted_iota(jnp.int32, (L, L), 1)
    causal = row >= col
    tril = jnp.where(causal, 1.0, 0.0).astype(BF16)
    d1, d2, d3 = _split3(da)
    a_cs = _dot(tril, d1) + _dot(tril, d2) + _dot(tril, d3)
    a_cs_t = a_cs.T
    dt_t = dt.T
    hg = SSM_HEADS // SSM_GROUPS
    gw = hg * SSM_HEAD_DIM
    for g in range(SSM_GROUPS):
        bg = bm[:, g * SSM_STATE:(g + 1) * SSM_STATE]
        cg = cm[:, g * SSM_STATE:(g + 1) * SSM_STATE]
        cb = _dot_nt(cg, bg)
        st_g = state_scr[g * gw:(g + 1) * gw, :]
        y_off = _dot_nt(cg, st_g.astype(BF16))
        for hh in range(hg):
            h = g * hg + hh
            hs = slice(h * SSM_HEAD_DIM, (h + 1) * SSM_HEAD_DIM)
            ac = a_cs[:, h:h + 1]
            ar = a_cs_t[h:h + 1, :]
            seg = jnp.exp(jnp.where(causal, ac - ar, -jnp.inf))
            wm = (cb * seg * dt_t[h:h + 1, :]).astype(BF16)
            xh = x[:, hs]
            yd = _dot(wm, xh.astype(BF16))
            yo = y_off[:, hh * SSM_HEAD_DIM:(hh + 1) * SSM_HEAD_DIM] * jnp.exp(ac)
            y_scr[:, hs] = yd + yo
            a_last = a_cs[L - 1:L, h:h + 1]
            xs_scr[:, hs] = xh * (jnp.exp(a_last - ac) * dt[:, h:h + 1])
        sts = _dot(xs_scr[:, g * gw:(g + 1) * gw].T.astype(BF16), bg)
        for hh in range(hg):
            h = g * hg + hh
            rs = slice(h * SSM_HEAD_DIM, (h + 1) * SSM_HEAD_DIM)
            a_last = a_cs[L - 1:L, h:h + 1]
            state_scr[rs, :] = (jnp.exp(a_last) * state_scr[rs, :]
                                + sts[hh * SSM_HEAD_DIM:(hh + 1) * SSM_HEAD_DIM, :])

    gated = (y_scr[...] + dsk_ref[...] * x) * _silu(zb_ref[...])
    ms = jnp.mean(gated * gated, axis=-1, keepdims=True)
    ob_ref[...] = (gated * lax.rsqrt(ms + EPS) * gs_ref[...]).astype(ob_ref.dtype)

    @pl.when(c == nc - 1)
    def _():
        hfin_ref[0] = state_scr[...]


def _ssd(xd_arr, xd_colblk, zb_arr, zb_colblk, prev8, h0, conv_w, conv_b, dt_bias, a_log, d_skip, g_ssm,
         bsz, nc, L, t_valid):
    rows = bsz * nc * L
    pad16 = lambda a: jnp.pad(a, (0, LANES - SSM_HEADS)).reshape(1, LANES)
    const = lambda shape: pl.BlockSpec(shape, lambda b, c: (0,) * len(shape))
    return pl.pallas_call(
        functools.partial(_ssd_kernel, L=L, t_valid=t_valid, nc=nc),
        out_shape=(jax.ShapeDtypeStruct((rows, W_BRANCH), BF16),
                   jax.ShapeDtypeStruct((bsz, SSM_HEADS * SSM_HEAD_DIM, SSM_STATE), F32)),
        grid=(bsz, nc),
        in_specs=[pl.BlockSpec((L, XD_WIDTH), lambda b, c: (b * nc + c, xd_colblk)),
                  pl.BlockSpec((L, W_BRANCH), lambda b, c: (b * nc + c, zb_colblk)),
                  pl.BlockSpec((1, SUBLANES, CONV_DIM), lambda b, c: (b, 0, 0)),
                  pl.BlockSpec((1, SSM_HEADS * SSM_HEAD_DIM, SSM_STATE), lambda b, c: (b, 0, 0)),
                  const((SSM_CONV, CONV_DIM)), const((1, CONV_DIM)), const((1, LANES)), const((1, LANES)),
                  const((1, W_BRANCH)), const((1, W_BRANCH))],
        out_specs=(pl.BlockSpec((L, W_BRANCH), lambda b, c: (b * nc + c, 0)),
                   pl.BlockSpec((1, SSM_HEADS * SSM_HEAD_DIM, SSM_STATE), lambda b, c: (b, 0, 0))),
        scratch_shapes=[pltpu.VMEM((SSM_HEADS * SSM_HEAD_DIM, SSM_STATE), F32),
                        pltpu.VMEM((SUBLANES, CONV_DIM), F32),
                        pltpu.VMEM((L, W_BRANCH), F32),
                        pltpu.VMEM((L, W_BRANCH), F32)],
        compiler_params=_params("arbitrary", "arbitrary"),
        name="ssd",
    )(xd_arr, zb_arr, prev8, h0, conv_w, conv_b.reshape(1, -1), pad16(dt_bias), pad16(a_log),
      jnp.repeat(d_skip, SSM_HEAD_DIM).reshape(1, -1), g_ssm.reshape(1, -1))


def _select_topk(gate, n_elig, axis):
    idx = lax.broadcasted_iota(jnp.int32, gate.shape, axis).astype(F32)
    big = jnp.float32(1 << 30)
    g = jnp.where(idx < n_elig.astype(F32), gate, -jnp.inf)
    sel = jnp.zeros(gate.shape, jnp.bool_)
    for _ in range(MOBA_TOPK):
        m = jnp.max(g, axis=axis, keepdims=True)
        hit = (g == m) & (m > -jnp.inf) & (m < jnp.inf)
        first = jnp.min(jnp.where(hit, idx, big), axis=axis, keepdims=True)
        pick = idx == first
        sel = sel | pick
        g = jnp.where(pick, -jnp.inf, g)
    return sel


def _attn_kernel(q_ref, k_ref, v_ref, zc_ref, sl_ref, o_ref, kt_ref, vt_ref, kbf, vtb, km_scr, *, T, tq):
    nb = T // MOBA_BLOCK
    nbp = km_scr.shape[0]
    qb = pl.program_id(2)

    @pl.when(qb == 0)
    def _():
        km_scr[...] = jnp.zeros(km_scr.shape, F32)
        for n in range(nb):
            rs = slice(n * MOBA_BLOCK, (n + 1) * MOBA_BLOCK)
            k = k_ref[rs, :]
            vt = v_ref[rs, :].T
            kt_ref[0, :, rs] = k.T
            vt_ref[0, :, rs] = vt
            kbf[n] = k.astype(BF16)
            vtb[n] = vt.astype(BF16)
            km_scr[n:n + 1, :] = jnp.mean(k, axis=0, keepdims=True)

    q = q_ref[...]
    lw = q.shape[1]
    lane = lax.broadcasted_iota(jnp.int32, (tq, lw), 1)
    krow = lax.broadcasted_iota(jnp.int32, (MOBA_BLOCK, tq), 0)
    qcol = lax.broadcasted_iota(jnp.int32, (MOBA_BLOCK, tq), 1)
    rel = (krow - qcol).astype(F32)
    brow = lax.broadcasted_iota(jnp.int32, (nbp, tq), 0)
    scale = ATT_HEAD_DIM ** -0.5
    k1, k2, _ = _split3(km_scr[...])
    nh = lw // ATT_HEAD_DIM
    qss, selbs, slopes, alibis = [], [], [], []
    for i in range(nh):
        qh = jnp.where(lane // ATT_HEAD_DIM == i, q, 0.0)
        q1, q2, _ = _split3(qh)
        gate = _dot_nt(k1, q1) + _dot_nt(k1, q2) + _dot_nt(k2, q1)
        selbs.append(jnp.where(_select_topk(gate, qb, 0), 0.0, NEG))
        qss.append((qh * scale).astype(BF16))
        slopes.append(sl_ref[0, :, i * ATT_HEAD_DIM:i * ATT_HEAD_DIM + 1])
        alibis.append(slopes[i] * rel)
    qs_all = jnp.concatenate(qss, axis=0)

    def logits(n):
        s_all = _dot_nt(kbf[n], qs_all)
        out = []
        for i in range(nh):
            rowterm = (jnp.sum(jnp.where(brow == n, selbs[i], 0.0), axis=0, keepdims=True)
                       + slopes[i] * ((n - qb) * MOBA_BLOCK).astype(F32))
            out.append(s_all[:, i * tq:(i + 1) * tq] + alibis[i] + rowterm)
        return out

    def update(i, blocks, carry):
        m, l, acc = carry
        m_new = m
        for _, s in blocks:
            m_new = jnp.maximum(m_new, jnp.max(s, axis=0, keepdims=True))
        alpha = jnp.exp(m - m_new)
        l = alpha * l
        acc = alpha * acc
        for n, s in blocks:
            p = jnp.exp(s - m_new)
            l = l + jnp.sum(p, axis=0, keepdims=True)
            acc = acc + _dot(vtb[n, pl.ds(i * ATT_HEAD_DIM, ATT_HEAD_DIM), :], p.astype(BF16))
        return m_new, l, acc

    def body(j, carry):
        n0 = 2 * j
        sa, sb = logits(n0), logits(n0 + 1)
        return tuple(update(i, [(n0, sa[i]), (n0 + 1, sb[i])], carry[i]) for i in range(nh))

    init = (jnp.full((1, tq), NEG, F32), jnp.zeros((1, tq), F32), jnp.zeros((ATT_HEAD_DIM, tq), F32))
    carry = lax.fori_loop(0, qb // 2, body, (init,) * nh)

    def finish(with_left):
        left = qb - 1
        s_left = logits(left) if with_left else None
        s_qk = _dot_nt(kbf[qb], qs_all)
        outs = []
        for i in range(nh):
            s_own = jnp.where(krow <= qcol, s_qk[:, i * tq:(i + 1) * tq] + alibis[i], NEG)
            blocks = ([(left, s_left[i])] if with_left else []) + [(qb, s_own)]
            m, l, acc = update(i, blocks, carry[i])
            outs.append(acc / l)
        return jnp.concatenate(outs, axis=0)

    o_t = lax.cond(qb % 2 == 1, lambda: finish(True), lambda: finish(False))
    o_ref[...] = (o_t.T * _silu(zc_ref[...])).astype(o_ref.dtype)


ATT_GROUP_HEADS = 4


def _alibi_slopes():
    return 2.0 ** (-8.0 * jnp.arange(1, ATT_HEADS + 1, dtype=F32) / ATT_HEADS)


def _attn_prompt(proj, bsz, T):
    tq = MOBA_BLOCK
    nq = T // tq
    nb = T // MOBA_BLOCK
    nbp = -(-nb // SUBLANES) * SUBLANES
    lw = ATT_GROUP_HEADS * ATT_HEAD_DIM
    hp_n = W_BRANCH // lw
    slopes = jnp.repeat(_alibi_slopes(), ATT_HEAD_DIM).reshape(hp_n, 1, lw)
    cpb = W_BRANCH // lw
    kv_t = lambda: pl.BlockSpec((1, lw, T), lambda b, hp, qi: (b, hp, 0))
    return pl.pallas_call(
        functools.partial(_attn_kernel, T=T, tq=tq),
        out_shape=(jax.ShapeDtypeStruct((bsz * T, W_BRANCH), BF16),
                   jax.ShapeDtypeStruct((bsz, W_BRANCH, T), F32),
                   jax.ShapeDtypeStruct((bsz, W_BRANCH, T), F32)),
        grid=(bsz, hp_n, nq),
        in_specs=[pl.BlockSpec((tq, lw), lambda b, hp, qi: (b * nq + qi, COL_Q * cpb + hp)),
                  pl.BlockSpec((T, lw), lambda b, hp, qi: (b, COL_K * cpb + hp)),
                  pl.BlockSpec((T, lw), lambda b, hp, qi: (b, COL_VA * cpb + hp)),
                  pl.BlockSpec((tq, lw), lambda b, hp, qi: (b * nq + qi, COL_ZC * cpb + hp)),
                  pl.BlockSpec((1, 1, lw), lambda b, hp, qi: (hp, 0, 0))],
        out_specs=(pl.BlockSpec((tq, lw), lambda b, hp, qi: (b * nq + qi, hp)), kv_t(), kv_t()),
        scratch_shapes=[pltpu.VMEM((nb, MOBA_BLOCK, lw), BF16), pltpu.VMEM((nb, lw, MOBA_BLOCK), BF16),
                        pltpu.VMEM((nbp, lw), F32)],
        compiler_params=_params("arbitrary", "arbitrary", "arbitrary"),
        name="attn_prompt",
    )(proj, proj, proj, proj, slopes)


PAGES_PER_STEP = 16
BLOCK_PAGES = MOBA_BLOCK // PAGE_SIZE


def _kmean_kernel(pt_ref, *refs):
    o_ref = refs[PAGES_PER_STEP]
    for i in range(PAGES_PER_STEP // BLOCK_PAGES):
        s = refs[i * BLOCK_PAGES][0]
        for j in range(1, BLOCK_PAGES):
            s = s + refs[i * BLOCK_PAGES + j][0]
        o_ref[0, i] = jnp.sum(s, axis=-1) * (1.0 / MOBA_BLOCK)


def _kmean_paged(cache_t, page_off, pt_flat, bsz, n_pages):
    steps = n_pages // PAGES_PER_STEP
    bps = PAGES_PER_STEP // BLOCK_PAGES

    def page_spec(i):
        return pl.BlockSpec((1, ATT_HEADS, ATT_HEAD_DIM, PAGE_SIZE),
                            lambda b, s, pt: (page_off + pt[b * n_pages + s * PAGES_PER_STEP + i], 0, 0, 0))

    return pl.pallas_call(
        _kmean_kernel,
        out_shape=jax.ShapeDtypeStruct((bsz, n_pages // BLOCK_PAGES, ATT_HEADS, ATT_HEAD_DIM), F32),
        grid_spec=pltpu.PrefetchScalarGridSpec(
            num_scalar_prefetch=1,
            grid=(bsz, steps),
            in_specs=[page_spec(i) for i in range(PAGES_PER_STEP)],
            out_specs=pl.BlockSpec((1, bps, ATT_HEADS, ATT_HEAD_DIM), lambda b, s, pt: (b, s, 0, 0))),
        compiler_params=_params("arbitrary", "arbitrary"),
        name="kmean_paged",
    )(pt_flat, *([cache_t] * PAGES_PER_STEP))


def _gate_topk_kernel(q_ref, km_ref, o_ref):
    gate = jnp.sum(km_ref[0] * q_ref[...], axis=-1)
    idx = lax.broadcasted_iota(jnp.int32, gate.shape, 0).astype(F32)
    rid = lax.broadcasted_iota(jnp.int32, (SUBLANES, ATT_HEADS), 0)
    big = jnp.float32(1 << 30)
    out = jnp.zeros((SUBLANES, ATT_HEADS), F32)
    g = gate
    for r in range(MOBA_TOPK):
        m = jnp.max(g, axis=0, keepdims=True)
        first = jnp.min(jnp.where(g == m, idx, big), axis=0, keepdims=True)
        out = jnp.where(rid == r, jnp.broadcast_to(first, (SUBLANES, ATT_HEADS)), out)
        g = jnp.where(idx == first, -jnp.inf, g)
    o_ref[0] = out.astype(jnp.int32)


def _gate_topk(q3, kmean):
    bsz, nblk = kmean.shape[:2]
    return pl.pallas_call(
        _gate_topk_kernel,
        out_shape=jax.ShapeDtypeStruct((bsz, SUBLANES, ATT_HEADS), jnp.int32),
        grid=(bsz,),
        in_specs=[pl.BlockSpec((1, ATT_HEADS, ATT_HEAD_DIM), lambda b: (b, 0, 0)),
                  pl.BlockSpec((1, nblk, ATT_HEADS, ATT_HEAD_DIM), lambda b: (b, 0, 0, 0))],
        out_specs=pl.BlockSpec((1, SUBLANES, ATT_HEADS), lambda b: (b, 0, 0)),
        compiler_params=_params("arbitrary"),
        name="gate_topk",
    )(q3, kmean)


N_SEL_TILES = MOBA_TOPK * BLOCK_PAGES


def _attn_sample_kernel(pt_ref, idx_ref, q_ref, kn_ref, vn_ref, zc_ref, sl_ref, *refs, past_len):
    nt = SAMPLE_GROUP_HEADS * N_SEL_TILES
    k_refs = refs[:nt]
    v_refs = refs[nt:2 * nt]
    o_ref = refs[2 * nt]
    b = pl.program_id(0)
    g = pl.program_id(1)
    scale = ATT_HEAD_DIM ** -0.5
    lane = lax.broadcasted_iota(jnp.int32, (SUBLANES, PAGE_SIZE), 1)
    bf = lambda a: a.astype(BF16).astype(F32)
    for hh in range(SAMPLE_GROUP_HEADS):
        h = g * SAMPLE_GROUP_HEADS + hh
        q = jnp.broadcast_to(q_ref[0, hh] * scale, (SUBLANES, ATT_HEAD_DIM)).astype(BF16)
        slope = sl_ref[hh]
        logits = []
        for j in range(MOBA_TOPK):
            blk = idx_ref[(b * MOBA_TOPK + j) * ATT_HEADS + h]
            for half in range(BLOCK_PAGES):
                kpos = blk * MOBA_BLOCK + half * PAGE_SIZE + lane
                kt = k_refs[hh * N_SEL_TILES + j * BLOCK_PAGES + half][0, 0]
                s = _dot(q, kt.astype(BF16))
                logits.append(s - slope * (past_len - kpos).astype(F32))
        s_own = jnp.sum(q.astype(F32) * bf(kn_ref[0, hh]), axis=1, keepdims=True)
        m = s_own
        for s in logits:
            m = jnp.maximum(m, jnp.max(s, axis=1, keepdims=True))
        p_own = jnp.exp(s_own - m)
        l = p_own
        acc = bf(p_own) * bf(vn_ref[0, hh])
        for t, s in enumerate(logits):
            p = jnp.exp(s - m)
            l = l + jnp.sum(p, axis=1, keepdims=True)
            acc = acc + _dot_nt(p.astype(BF16), v_refs[hh * N_SEL_TILES + t][0, 0].astype(BF16))
        o = (acc / l) * _silu(zc_ref[0, hh])
        o_ref[0, hh] = o[0:1, :]


SAMPLE_GROUP_HEADS = 4


def _attn_sample(cache_kt, cache_vt, page_off, pt_flat, idx_flat, q4, kn4, vn4, zc4, n_pages):
    bsz = q4.shape[0]
    gh = SAMPLE_GROUP_HEADS
    past_len = n_pages * PAGE_SIZE
    slopes = jnp.broadcast_to(_alibi_slopes()[:, None, None], (ATT_HEADS, 1, PAGE_SIZE))

    def tile_spec(hh, j, half):
        def imap(b, g, pt, idx):
            h = g * gh + hh
            blk = idx[(b * MOBA_TOPK + j) * ATT_HEADS + h]
            return (page_off + pt[b * n_pages + blk * BLOCK_PAGES + half], h, 0, 0)
        return pl.BlockSpec((1, 1, ATT_HEAD_DIM, PAGE_SIZE), imap)

    tiles = [tile_spec(hh, j, half) for hh in range(gh) for j in range(MOBA_TOPK) for half in range(BLOCK_PAGES)]
    head = lambda: pl.BlockSpec((1, gh, 1, ATT_HEAD_DIM), lambda b, g, pt, idx: (b, g, 0, 0))
    return pl.pallas_call(
        functools.partial(_attn_sample_kernel, past_len=past_len),
        out_shape=jax.ShapeDtypeStruct((bsz, ATT_HEADS, 1, ATT_HEAD_DIM), F32),
        grid_spec=pltpu.PrefetchScalarGridSpec(
            num_scalar_prefetch=2,
            grid=(bsz, ATT_HEADS // gh),
            in_specs=[head(), head(), head(), head(),
                      pl.BlockSpec((gh, 1, PAGE_SIZE), lambda b, g, pt, idx: (g, 0, 0))] + tiles + tiles,
            out_specs=head()),
        compiler_params=_params("arbitrary", "arbitrary"),
        name="attn_sample",
    )(pt_flat, idx_flat, q4, kn4, vn4, zc4, slopes,
      *([cache_kt] * len(tiles)), *([cache_vt] * len(tiles)))


def _merge_kernel(oa_ref, ob_ref, oc_ref, ga_ref, gb_ref, gc_ref, w_ref, o_ref):
    acc = _sigmoid(ga_ref[...]) * _dot(oa_ref[...], w_ref[0])
    acc = acc + _sigmoid(gb_ref[...]) * _dot(ob_ref[...], w_ref[1])
    acc = acc + _sigmoid(gc_ref[...]) * _dot(oc_ref[...], w_ref[2])
    o_ref[...] = acc.astype(o_ref.dtype)


def _merge(o_a, o_b, o_c, proj, w_branch, tm, tn):
    m = o_a.shape[0]
    nt = D_MODEL // tn
    gate_blk0 = GATE_COLBLK * D_MODEL // tn
    br = lambda: pl.BlockSpec((tm, W_BRANCH), lambda j, i: (i, 0))
    gate = lambda r: pl.BlockSpec((tm, tn), lambda j, i: (i, gate_blk0 + r * nt + j))
    return pl.pallas_call(
        _merge_kernel,
        out_shape=jax.ShapeDtypeStruct((m, D_MODEL), BF16),
        grid=(nt, m // tm),
        in_specs=[br(), br(), br(), gate(0), gate(1), gate(2),
                  pl.BlockSpec((N_BRANCH, W_BRANCH, tn), lambda j, i: (0, 0, j))],
        out_specs=pl.BlockSpec((tm, tn), lambda j, i: (i, j)),
        compiler_params=_params("parallel", "parallel"),
        name="merge",
    )(o_a, o_b, o_c, proj, proj, proj, w_branch)


def _outproj_kernel(m_ref, w_ref, x_ref, g_ref, gn_ref, o_ref, h_ref):
    out = _dot(m_ref[...], w_ref[...])
    ms = jnp.mean(out * out, axis=-1, keepdims=True)
    y = x_ref[...] + out * lax.rsqrt(ms + EPS) * g_ref[...]
    o_ref[...] = y
    ms_y = jnp.mean(y * y, axis=-1, keepdims=True)
    h_ref[...] = (y * lax.rsqrt(ms_y + EPS) * gn_ref[...]).astype(h_ref.dtype)


def _outproj(merged, w_out, x, g_post, g_next, tm):
    m = merged.shape[0]
    rows = lambda: pl.BlockSpec((tm, D_MODEL), lambda i: (i, 0))
    gain = lambda: pl.BlockSpec((1, D_MODEL), lambda i: (0, 0))
    return pl.pallas_call(
        _outproj_kernel,
        out_shape=(jax.ShapeDtypeStruct((m, D_MODEL), F32), jax.ShapeDtypeStruct((m, D_MODEL), BF16)),
        grid=(m // tm,),
        in_specs=[rows(), pl.BlockSpec((D_MODEL, D_MODEL), lambda i: (0, 0)), rows(), gain(), gain()],
        out_specs=(rows(), rows()),
        compiler_params=_params("parallel"),
        name="outproj",
    )(merged, w_out, x, g_post.reshape(1, -1), g_next.reshape(1, -1))


def _reorder_w_in(w):
    u, v, za, xbc, zb, dt, q, k, va, zc, gl = jnp.split(w.T, [int(i) for i in np.cumsum(IN_SIZES)[:-1]], axis=0)
    pad = jnp.zeros((XD_WIDTH - CONV_DIM - SSM_HEADS, w.shape[0]), w.dtype)
    return jnp.concatenate([u, v, za, zb, q, k, va, zc, gl, xbc, dt, pad], axis=0).astype(BF16)


def _seg(proj, blk, rows=None):
    out = proj[:, blk * W_BRANCH:(blk + 1) * W_BRANCH]
    return out if rows is None else out[:rows]


def _prompt_layer(x, h, bsz, T, lw, g_next):
    (_, w_in, w_s, b_s, ln_g, ln_b, conv_w, conv_b, dt_bias, a_log, d_skip, g_ssm, w_br, w_out, g_post) = lw
    proj = _inproj(h, w_in, 512, 1024)
    o_a = _gmlp(proj, w_s, b_s, ln_g, ln_b, 512)
    L = min(SSM_CHUNK, T)
    nc = T // L
    prev8 = jnp.zeros((bsz, SUBLANES, CONV_DIM), F32)
    h0 = jnp.zeros((bsz, SSM_HEADS * SSM_HEAD_DIM, SSM_STATE), F32)
    o_b, h_fin = _ssd(proj, XD_COLBLK, proj, COL_ZB, prev8, h0, conv_w, conv_b, dt_bias, a_log, d_skip, g_ssm,
                      bsz, nc, L, T)
    o_c, k_t, v_t = _attn_prompt(proj, bsz, T)
    merged = _merge(o_a, o_b, o_c, proj, w_br, 512, 1024)
    y, h_next = _outproj(merged, w_out, x, g_post, g_next, 256)
    k_new = k_t.reshape(bsz, ATT_HEADS, ATT_HEAD_DIM, T).transpose(0, 3, 1, 2)
    v_new = v_t.reshape(bsz, ATT_HEADS, ATT_HEAD_DIM, T).transpose(0, 3, 1, 2)
    xbc_raw = proj[:, XD_COLBLK * XD_WIDTH:XD_COLBLK * XD_WIDTH + CONV_DIM].reshape(bsz, T, CONV_DIM)
    conv_new = xbc_raw[:, T - (SSM_CONV - 1):]
    ssm_new = h_fin.reshape(bsz, SSM_HEADS, SSM_HEAD_DIM, SSM_STATE)
    return y, h_next, k_new, v_new, ssm_new, conv_new


SAMPLE_ROWS = 16
SAMPLE_L = 128


def _sample_layer(x16, h, bsz, conv_prev, ssm_prev, cache_kt, cache_vt, page_off, pt_flat, n_pages, lw, g_next):
    (_, w_in, w_s, b_s, ln_g, ln_b, conv_w, conv_b, dt_bias, a_log, d_skip, g_ssm, w_br, w_out, g_post) = lw
    proj = _inproj(h, w_in, SAMPLE_ROWS, 1024)
    o_a, v_a = _gmlp_single(proj, w_s, b_s, ln_g, ln_b)
    xd = proj[:bsz, XD_COLBLK * XD_WIDTH:]
    xd_rows = jnp.zeros((bsz, SAMPLE_L, XD_WIDTH), F32).at[:, 0].set(xd).reshape(bsz * SAMPLE_L, XD_WIDTH)
    zb_rows = jnp.zeros((bsz, SAMPLE_L, W_BRANCH), F32).at[:, 0].set(_seg(proj, COL_ZB, bsz))
    zb_rows = zb_rows.reshape(bsz * SAMPLE_L, W_BRANCH)
    prev8 = jnp.concatenate([jnp.zeros((bsz, SUBLANES - (SSM_CONV - 1), CONV_DIM), F32), conv_prev], axis=1)
    h0 = ssm_prev.reshape(bsz, SSM_HEADS * SSM_HEAD_DIM, SSM_STATE)
    ob_rows, h_fin = _ssd(xd_rows, 0, zb_rows, 0, prev8, h0, conv_w, conv_b, dt_bias, a_log, d_skip, g_ssm,
                          bsz, 1, SAMPLE_L, 1)
    o_b = ob_rows.reshape(bsz, SAMPLE_L, W_BRANCH)[:, 0]
    o_b = jnp.pad(o_b, ((0, SAMPLE_ROWS - bsz), (0, 0)))
    conv_new = jnp.concatenate([conv_prev[:, 1:], xd[:, None, :CONV_DIM]], axis=1)
    ssm_new = h_fin.reshape(bsz, SSM_HEADS, SSM_HEAD_DIM, SSM_STATE)
    q = _seg(proj, COL_Q, bsz)
    k_new = _seg(proj, COL_K, bsz)
    v_new = _seg(proj, COL_VA, bsz)
    kmean = _kmean_paged(cache_kt, page_off, pt_flat, bsz, n_pages)
    top = _gate_topk(q.reshape(bsz, ATT_HEADS, ATT_HEAD_DIM), kmean)
    idx_flat = top[:, :MOBA_TOPK].reshape(-1)
    heads4 = lambda a: a.reshape(bsz, ATT_HEADS, 1, ATT_HEAD_DIM)
    att = _attn_sample(cache_kt, cache_vt, page_off, pt_flat, idx_flat, heads4(q), heads4(k_new), heads4(v_new),
                       heads4(_seg(proj, COL_ZC, bsz)), n_pages)
    o_c = jnp.pad(att.reshape(bsz, W_BRANCH), ((0, SAMPLE_ROWS - bsz), (0, 0))).astype(BF16)
    merged = _merge(o_a, o_b, o_c, proj, w_br, SAMPLE_ROWS, 1024)
    y, h_next = _outproj(merged, w_out, x16, g_post, g_next, SAMPLE_ROWS)
    shape4 = (bsz, 1, ATT_HEADS, ATT_HEAD_DIM)
    return (y, h_next, k_new.reshape(shape4), v_new.reshape(shape4), ssm_new, conv_new,
            v_a[:bsz].reshape(bsz, 1, W_BRANCH))


def kernel(x_prompt, x_sample, cache_k, cache_v, state_ssm, state_conv, page_table, g_pre, w_in, w_spatial,
           b_spatial, ln_v_g, ln_v_b, conv_w, conv_b, dt_bias, a_log, d_skip, g_ssm, w_branch, w_out, g_post):
    depth = w_in.shape[0]
    bp, T, _ = x_prompt.shape
    bs = x_sample.shape[0]
    n_phys = cache_k.shape[1]
    n_pages = page_table.shape[1]
    pt_flat = page_table.reshape(-1).astype(jnp.int32)
    xp = x_prompt.reshape(bp * T, D_MODEL)
    xs = jnp.pad(x_sample.reshape(bs, D_MODEL), ((0, SAMPLE_ROWS - bs), (0, 0)))
    to_t = lambda c: c.transpose(0, 1, 3, 4, 2).reshape(depth * n_phys, ATT_HEADS, ATT_HEAD_DIM, PAGE_SIZE)
    cache_kt, cache_vt = to_t(cache_k), to_t(cache_v)
    outs = [[] for _ in range(9)]
    hp = _prenorm(xp, g_pre[0], 256)
    hs = _prenorm(xs, g_pre[0], SAMPLE_ROWS)
    for l in range(depth):
        lw = (g_pre[l], _reorder_w_in(w_in[l]), w_spatial[l], b_spatial[l], ln_v_g[l], ln_v_b[l], conv_w[l],
              conv_b[l], dt_bias[l], a_log[l], d_skip[l], g_ssm[l], w_branch[l].astype(BF16),
              w_out[l].astype(BF16), g_post[l])
        g_next = g_pre[(l + 1) % depth]
        xp, hp, kp, vp, sp, cp = _prompt_layer(xp, hp, bp, T, lw, g_next)
        xs, hs, ks, vs, ss, cs, gv = _sample_layer(xs, hs, bs, state_conv[l], state_ssm[l], cache_kt, cache_vt,
                                                   l * n_phys, pt_flat, n_pages, lw, g_next)
        for lst, val in zip(outs, (kp, vp, sp, cp, ks, vs, ss, cs, gv)):
            lst.append(val)
    return (xp.reshape(bp, T, D_MODEL), xs[:bs].reshape(bs, 1, D_MODEL)) + tuple(jnp.stack(o) for o in outs)
```

```python
import functools

import jax
import jax.numpy as jnp
import numpy as np
from jax import lax
from jax.experimental import pallas as pl
from jax.experimental.pallas import tpu as pltpu

F32 = jnp.float32
BF16 = jnp.bfloat16

D_MODEL = 2048
W_BRANCH = 1024
N_BRANCH = 3
GMLP_GROUPS = 4
GMLP_CHUNK = 128
GMLP_GW = W_BRANCH // GMLP_GROUPS
SSM_HEAD_DIM = 64
SSM_HEADS = 16
SSM_GROUPS = 2
SSM_STATE = 128
SSM_CONV = 4
SSM_CHUNK = 256
CONV_DIM = W_BRANCH + 2 * SSM_GROUPS * SSM_STATE
ATT_HEAD_DIM = 64
ATT_HEADS = 16
MOBA_BLOCK = 256
MOBA_TOPK = 3
PAGE_SIZE = 128
EPS = 1e-6
IN_SIZES = [W_BRANCH, W_BRANCH, W_BRANCH, CONV_DIM, W_BRANCH, SSM_HEADS,
            W_BRANCH, W_BRANCH, W_BRANCH, W_BRANCH, N_BRANCH * D_MODEL]

LANES = 128
SUBLANES = 8
VMEM_LIMIT = 56 * 1024 * 1024

COL_U, COL_V, COL_ZA, COL_ZB, COL_Q, COL_K, COL_VA, COL_ZC = range(8)
GATE_COLBLK = 4
XD_COLBLK = 7
XD_WIDTH = D_MODEL
N_PROJ = 8 * W_BRANCH + N_BRANCH * D_MODEL + XD_WIDTH
NEG = -1e30


def _params(*sem):
    return pltpu.CompilerParams(dimension_semantics=sem, vmem_limit_bytes=VMEM_LIMIT)


def _sigmoid(x):
    return 1.0 / (1.0 + jnp.exp(-x))


def _silu(x):
    return x * _sigmoid(x)


def _gelu(x):
    return 0.5 * x * (1.0 + jnp.tanh(0.7978845608028654 * (x + 0.044715 * (x * x * x))))


def _softplus(x):
    return jnp.maximum(x, 0.0) + jnp.log1p(jnp.exp(-jnp.abs(x)))


def _dot(a, b):
    return jnp.dot(a, b, preferred_element_type=F32)


def _dot_nt(a, b):
    return lax.dot_general(a, b, (((1,), (1,)), ((), ())), preferred_element_type=F32)


def _split3(x):
    x1 = x.astype(BF16)
    r1 = x - x1.astype(F32)
    x2 = r1.astype(BF16)
    r2 = r1 - x2.astype(F32)
    return x1, x2, r2.astype(BF16)


def _prenorm_kernel(x_ref, g_ref, o_ref):
    x = x_ref[...]
    ms = jnp.mean(x * x, axis=-1, keepdims=True)
    o_ref[...] = (x * lax.rsqrt(ms + EPS) * g_ref[...]).astype(o_ref.dtype)


def _prenorm(x, g, tm):
    m, d = x.shape
    return pl.pallas_call(
        _prenorm_kernel,
        out_shape=jax.ShapeDtypeStruct((m, d), BF16),
        grid=(m // tm,),
        in_specs=[pl.BlockSpec((tm, d), lambda i: (i, 0)),
                  pl.BlockSpec((1, d), lambda i: (0, 0))],
        out_specs=pl.BlockSpec((tm, d), lambda i: (i, 0)),
        compiler_params=_params("parallel"),
        name="prenorm",
    )(x, g.reshape(1, d))


W_HALF = 512
W_ROW_ALIGN = 16
PROJ_TN = 2 * W_HALF
BLOCK_PAGES = MOBA_BLOCK // PAGE_SIZE


def _w_in_row_offsets():
    start = dict(zip("u v za xbc zb dt q k va zc gl".split(), [0] + [int(c) for c in np.cumsum(IN_SIZES)[:-1]]))
    offs = []
    for name in "u v za zb q k va zc".split():
        offs += [start[name] + t * W_HALF for t in range(W_BRANCH // W_HALF)]
    offs += [start["gl"] + t * W_HALF for t in range(N_BRANCH * D_MODEL // W_HALF)]
    offs += [start["xbc"] + t * W_HALF for t in range(CONV_DIM // W_HALF)]
    offs += [start["dt"]]
    assert len(offs) * W_HALF == N_PROJ and max(offs) + W_HALF <= sum(IN_SIZES)
    assert all(o % W_ROW_ALIGN == 0 for o in offs)
    return jnp.asarray([o // W_ROW_ALIGN for o in offs], jnp.int32)


def _block_mean_keys(page_refs, km_ref):
    for i in range(len(page_refs) // BLOCK_PAGES):
        s = page_refs[i * BLOCK_PAGES][0]
        for j in range(1, BLOCK_PAGES):
            s = s + page_refs[i * BLOCK_PAGES + j][0]
        km_ref[0, i] = jnp.sum(s, axis=-1) * (1.0 / MOBA_BLOCK)


def _inproj_kernel(offs_ref, pt_ref, x_ref, w0_ref, w1_ref, *refs):
    n_pg = len(refs) - 3 if len(refs) > 2 else 0
    page_refs, o_ref, wbf = refs[:n_pg], refs[n_pg], refs[-1]

    @pl.when(pl.program_id(1) == 0)
    def _():
        wbf[0:W_HALF, :] = w0_ref[...].astype(BF16)
        wbf[W_HALF:, :] = w1_ref[...].astype(BF16)

    o_ref[...] = _dot_nt(x_ref[...], wbf[...])
    if n_pg:
        _block_mean_keys(page_refs, refs[n_pg + 1])


def _inproj(h, w_t_all, layer, tm, pool=None):
    m, k = h.shape
    nj, ni = N_PROJ // PROJ_TN, m // tm
    half = lambda t: pl.BlockSpec((pl.Squeezed(), pl.Element(W_HALF), pl.Element(k)),
                                  lambda j, i, offs, pt: (layer, offs[2 * j + t] * W_ROW_ALIGN, 0))
    in_specs = [pl.BlockSpec((tm, k), lambda j, i, offs, pt: (i, 0)), half(0), half(1)]
    out_shape = [jax.ShapeDtypeStruct((m, N_PROJ), F32)]
    out_specs = [pl.BlockSpec((tm, PROJ_TN), lambda j, i, offs, pt: (i, j))]
    operands = [h, w_t_all, w_t_all]
    pt_flat = jnp.zeros((1,), jnp.int32)
    if pool is not None:
        cache_t, page_off, pt_flat, n_seq, n_pages = pool
        pps = n_seq * n_pages // (nj * ni)
        bps = pps // BLOCK_PAGES
        assert pps * nj * ni == n_seq * n_pages and bps * BLOCK_PAGES == pps and (n_pages // BLOCK_PAGES) % bps == 0
        spb = n_pages // pps

        def page_spec(t):
            return pl.BlockSpec((1, ATT_HEADS, ATT_HEAD_DIM, PAGE_SIZE),
                                lambda j, i, offs, pt: (page_off + pt[(j * ni + i) * pps + t], 0, 0, 0))

        in_specs += [page_spec(t) for t in range(pps)]
        operands += [cache_t] * pps
        out_shape.append(jax.ShapeDtypeStruct((n_seq, n_pages // BLOCK_PAGES, ATT_HEADS, ATT_HEAD_DIM), F32))
        out_specs.append(pl.BlockSpec((1, bps, ATT_HEADS, ATT_HEAD_DIM),
                                      lambda j, i, offs, pt: ((j * ni + i) // spb, (j * ni + i) % spb, 0, 0)))
    res = pl.pallas_call(
        _inproj_kernel,
        out_shape=tuple(out_shape),
        grid_spec=pltpu.PrefetchScalarGridSpec(
            num_scalar_prefetch=2,
            grid=(nj, ni),
            in_specs=in_specs,
            out_specs=tuple(out_specs),
            scratch_shapes=[pltpu.VMEM((PROJ_TN, k), BF16)]),
        compiler_params=_params("arbitrary", "arbitrary"),
        name="inproj",
    )(_w_in_row_offsets(), pt_flat, *operands)
    return res if pool is not None else res[0]


def _layernorm_rows(x, g, b):
    mu = jnp.mean(x, axis=-1, keepdims=True)
    d = x - mu
    var = jnp.mean(d * d, axis=-1, keepdims=True)
    return d * lax.rsqrt(var + EPS) * g + b


def _gmlp_kernel(u_ref, v_ref, z_ref, ws_ref, bt_ref, lg_ref, lb_ref, o_ref, *, cc, nch):
    row = lax.broadcasted_iota(jnp.int32, (cc, cc), 0)
    col = lax.broadcasted_iota(jnp.int32, (cc, cc), 1)
    tri = row >= col
    for c in range(nch):
        sl = pl.ds(c * cc, cc)
        va = _layernorm_rows(_gelu(v_ref[sl, :]), lg_ref[...], lb_ref[...])
        for g in range(GMLP_GROUPS):
            gs = slice(g * GMLP_GW, (g + 1) * GMLP_GW)
            wc = jnp.where(tri, ws_ref[g], 0.0).astype(BF16)
            mixed = _dot(wc, va[:, gs].astype(BF16)) + bt_ref[:, g:g + 1]
            o = _gelu(u_ref[sl, gs]) * mixed * _silu(z_ref[sl, gs])
            o_ref[sl, gs] = o.astype(o_ref.dtype)


def _gmlp(proj, w_s, b_s, ln_g, ln_b, tr):
    m = proj.shape[0]
    cc = GMLP_CHUNK
    return pl.pallas_call(
        functools.partial(_gmlp_kernel, cc=cc, nch=tr // cc),
        out_shape=jax.ShapeDtypeStruct((m, W_BRANCH), BF16),
        grid=(m // tr,),
        in_specs=[pl.BlockSpec((tr, W_BRANCH), lambda i: (i, COL_U)),
                  pl.BlockSpec((tr, W_BRANCH), lambda i: (i, COL_V)),
                  pl.BlockSpec((tr, W_BRANCH), lambda i: (i, COL_ZA)),
                  pl.BlockSpec((GMLP_GROUPS, cc, cc), lambda i: (0, 0, 0)),
                  pl.BlockSpec((cc, GMLP_GROUPS), lambda i: (0, 0)),
                  pl.BlockSpec((1, W_BRANCH), lambda i: (0, 0)),
                  pl.BlockSpec((1, W_BRANCH), lambda i: (0, 0))],
        out_specs=pl.BlockSpec((tr, W_BRANCH), lambda i: (i, 0)),
        compiler_params=_params("parallel"),
        name="gmlp",
    )(proj, proj, proj, w_s, b_s.T, ln_g.reshape(1, -1), ln_b.reshape(1, -1))


def _gmlp1_kernel(u_ref, v_ref, z_ref, w0_ref, b0_ref, lg_ref, lb_ref, o_ref, va_ref):
    va = _layernorm_rows(_gelu(v_ref[...]), lg_ref[...], lb_ref[...])
    va_ref[...] = va
    mixed = w0_ref[...] * va + b0_ref[...]
    o_ref[...] = (_gelu(u_ref[...]) * mixed * _silu(z_ref[...])).astype(o_ref.dtype)


def _gmlp_single(proj, w_s, b_s, ln_g, ln_b):
    m = proj.shape[0]
    w0 = jnp.repeat(w_s[:, 0, 0], GMLP_GW).reshape(1, W_BRANCH)
    b0 = jnp.repeat(b_s[:, 0], GMLP_GW).reshape(1, W_BRANCH)
    row = lambda: pl.BlockSpec((1, W_BRANCH), lambda i: (0, 0))
    return pl.pallas_call(
        _gmlp1_kernel,
        out_shape=(jax.ShapeDtypeStruct((m, W_BRANCH), BF16),
                   jax.ShapeDtypeStruct((m, W_BRANCH), F32)),
        grid=(1,),
        in_specs=[pl.BlockSpec((m, W_BRANCH), lambda i: (0, COL_U)),
                  pl.BlockSpec((m, W_BRANCH), lambda i: (0, COL_V)),
                  pl.BlockSpec((m, W_BRANCH), lambda i: (0, COL_ZA)),
                  row(), row(), row(), row()],
        out_specs=(pl.BlockSpec((m, W_BRANCH), lambda i: (0, 0)),
                   pl.BlockSpec((m, W_BRANCH), lambda i: (0, 0))),
        compiler_params=_params("arbitrary"),
        name="gmlp_single",
    )(proj, proj, proj, w0, b0, ln_g.reshape(1, -1), ln_b.reshape(1, -1))


def _ssd_kernel(xd_ref, zb_ref, prev_ref, h0_ref, cw_ref, cb_ref, dtb_ref, alog_ref, dsk_ref, gs_ref,
                ob_ref, hfin_ref, state_scr, tail_scr, y_scr, xs_scr, *, L, t_valid, nc):
    c = pl.program_id(1)

    @pl.when(c == 0)
    def _():
        state_scr[...] = h0_ref[0]
        tail_scr[...] = prev_ref[0]

    raw = xd_ref[:, :CONV_DIM]
    tail = tail_scr[...]
    rid = lax.broadcasted_iota(jnp.int32, (SUBLANES, CONV_DIM), 0)
    acc = cb_ref[...] + raw * cw_ref[SSM_CONV - 1:SSM_CONV, :]
    for k in range(1, SSM_CONV):
        rolled = pltpu.roll(raw, k, axis=0)
        head = jnp.where(rid < k, pltpu.roll(tail, k, axis=0), rolled[:SUBLANES])
        shifted = jnp.concatenate([head, rolled[SUBLANES:]], axis=0)
        acc = acc + shifted * cw_ref[SSM_CONV - 1 - k:SSM_CONV - k, :]
    tail_scr[...] = raw[L - SUBLANES:, :]
    xbc = _silu(acc)
    x = xbc[:, :W_BRANCH]
    gn = SSM_GROUPS * SSM_STATE
    bm = xbc[:, W_BRANCH:W_BRANCH + gn].astype(BF16)
    cm = xbc[:, W_BRANCH + gn:].astype(BF16)

    dt = _softplus(xd_ref[:, CONV_DIM:CONV_DIM + LANES] + dtb_ref[...])
    if t_valid < nc * L:
        grow = c * L + lax.broadcasted_iota(jnp.int32, (L, LANES), 0)
        dt = jnp.where(grow < t_valid, dt, 0.0)
    da = dt * (-jnp.exp(alog_ref[...]))
    row = lax.broadcasted_iota(jnp.int32, (L, L), 0)
    col = lax.broadcasted_iota(jnp.int32, (L, L), 1)
    causal = row >= col
    tril = jnp.where(causal, 1.0, 0.0).astype(BF16)
    d1, d2, d3 = _split3(da)
    a_cs = _dot(tril, d1) + _dot(tril, d2) + _dot(tril, d3)
    a_cs_t = a_cs.T
    dt_t = dt.T
    hg = SSM_HEADS // SSM_GROUPS
    gw = hg * SSM_HEAD_DIM
    for g in range(SSM_GROUPS):
        bg = bm[:, g * SSM_STATE:(g + 1) * SSM_STATE]
        cg = cm[:, g * SSM_STATE:(g + 1) * SSM_STATE]
        cb = _dot_nt(cg, bg)
        st_g = state_scr[g * gw:(g + 1) * gw, :]
        y_off = _dot_nt(cg, st_g.astype(BF16))
        for hh in range(hg):
            h = g * hg + hh
            hs = slice(h * SSM_HEAD_DIM, (h + 1) * SSM_HEAD_DIM)
            ac = a_cs[:, h:h + 1]
            ar = a_cs_t[h:h + 1, :]
            seg = jnp.exp(jnp.where(causal, ac - ar, -jnp.inf))
            wm = (cb * seg * dt_t[h:h + 1, :]).astype(BF16)
            xh = x[:, hs]
            yd = _dot(wm, xh.astype(BF16))
            yo = y_off[:, hh * SSM_HEAD_DIM:(hh + 1) * SSM_HEAD_DIM] * jnp.exp(ac)
            y_scr[:, hs] = yd + yo
            a_last = a_cs[L - 1:L, h:h + 1]
            xs_scr[:, hs] = xh * (jnp.exp(a_last - ac) * dt[:, h:h + 1])
        sts = _dot(xs_scr[:, g * gw:(g + 1) * gw].T.astype(BF16), bg)
        for hh in range(hg):
            h = g * hg + hh
            rs = slice(h * SSM_HEAD_DIM, (h + 1) * SSM_HEAD_DIM)
            a_last = a_cs[L - 1:L, h:h + 1]
            state_scr[rs, :] = (jnp.exp(a_last) * state_scr[rs, :]
                                + sts[hh * SSM_HEAD_DIM:(hh + 1) * SSM_HEAD_DIM, :])

    gated = (y_scr[...] + dsk_ref[...] * x) * _silu(zb_ref[...])
    ms = jnp.mean(gated * gated, axis=-1, keepdims=True)
    ob_ref[...] = (gated * lax.rsqrt(ms + EPS) * gs_ref[...]).astype(ob_ref.dtype)

    @pl.when(c == nc - 1)
    def _():
        hfin_ref[0] = state_scr[...]


def _ssd(xd_arr, xd_colblk, zb_arr, zb_colblk, prev8, h0, conv_w, conv_b, dt_bias, a_log, d_skip, g_ssm,
         bsz, nc, L, t_valid):
    rows = bsz * nc * L
    pad16 = lambda a: jnp.pad(a, (0, LANES - SSM_HEADS)).reshape(1, LANES)
    const = lambda shape: pl.BlockSpec(shape, lambda b, c: (0,) * len(shape))
    return pl.pallas_call(
        functools.partial(_ssd_kernel, L=L, t_valid=t_valid, nc=nc),
        out_shape=(jax.ShapeDtypeStruct((rows, W_BRANCH), BF16),
                   jax.ShapeDtypeStruct((bsz, SSM_HEADS * SSM_HEAD_DIM, SSM_STATE), F32)),
        grid=(bsz, nc),
        in_specs=[pl.BlockSpec((L, XD_WIDTH), lambda b, c: (b * nc + c, xd_colblk)),
                  pl.BlockSpec((L, W_BRANCH), lambda b, c: (b * nc + c, zb_colblk)),
                  pl.BlockSpec((1, SUBLANES, CONV_DIM), lambda b, c: (b, 0, 0)),
                  pl.BlockSpec((1, SSM_HEADS * SSM_HEAD_DIM, SSM_STATE), lambda b, c: (b, 0, 0)),
                  const((SSM_CONV, CONV_DIM)), const((1, CONV_DIM)), const((1, LANES)), const((1, LANES)),
                  const((1, W_BRANCH)), const((1, W_BRANCH))],
        out_specs=(pl.BlockSpec((L, W_BRANCH), lambda b, c: (b * nc + c, 0)),
                   pl.BlockSpec((1, SSM_HEADS * SSM_HEAD_DIM, SSM_STATE), lambda b, c: (b, 0, 0))),
        scratch_shapes=[pltpu.VMEM((SSM_HEADS * SSM_HEAD_DIM, SSM_STATE), F32),
                        pltpu.VMEM((SUBLANES, CONV_DIM), F32),
                        pltpu.VMEM((L, W_BRANCH), F32),
                        pltpu.VMEM((L, W_BRANCH), F32)],
        compiler_params=_params("arbitrary", "arbitrary"),
        name="ssd",
    )(xd_arr, zb_arr, prev8, h0, conv_w, conv_b.reshape(1, -1), pad16(dt_bias), pad16(a_log),
      jnp.repeat(d_skip, SSM_HEAD_DIM).reshape(1, -1), g_ssm.reshape(1, -1))


def _select_topk(gate, n_elig, axis):
    idx = lax.broadcasted_iota(jnp.int32, gate.shape, axis).astype(F32)
    big = jnp.float32(1 << 30)
    g = jnp.where(idx < n_elig.astype(F32), gate, -jnp.inf)
    sel = jnp.zeros(gate.shape, jnp.bool_)
    for _ in range(MOBA_TOPK):
        m = jnp.max(g, axis=axis, keepdims=True)
        hit = (g == m) & (m > -jnp.inf) & (m < jnp.inf)
        first = jnp.min(jnp.where(hit, idx, big), axis=axis, keepdims=True)
        pick = idx == first
        sel = sel | pick
        g = jnp.where(pick, -jnp.inf, g)
    return sel


def _attn_kernel(q_ref, k_ref, v_ref, zc_ref, sl_ref, o_ref, kt_ref, vt_ref, kbf, vtb, km_scr, *, T, tq):
    nb = T // MOBA_BLOCK
    nbp = km_scr.shape[0]
    qb = pl.program_id(2)

    @pl.when(qb == 0)
    def _():
        km_scr[...] = jnp.zeros(km_scr.shape, F32)
        for n in range(nb):
            rs = slice(n * MOBA_BLOCK, (n + 1) * MOBA_BLOCK)
            k = k_ref[rs, :]
            vt = v_ref[rs, :].T
            kt_ref[0, :, rs] = k.T
            vt_ref[0, :, rs] = vt
            kbf[n] = k.astype(BF16)
            vtb[n] = vt.astype(BF16)
            km_scr[n:n + 1, :] = jnp.mean(k, axis=0, keepdims=True)

    q = q_ref[...]
    lw = q.shape[1]
    lane = lax.broadcasted_iota(jnp.int32, (tq, lw), 1)
    krow = lax.broadcasted_iota(jnp.int32, (MOBA_BLOCK, tq), 0)
    qcol = lax.broadcasted_iota(jnp.int32, (MOBA_BLOCK, tq), 1)
    rel = (krow - qcol).astype(F32)
    brow = lax.broadcasted_iota(jnp.int32, (nbp, tq), 0)
    scale = ATT_HEAD_DIM ** -0.5
    k1, k2, _ = _split3(km_scr[...])
    nh = lw // ATT_HEAD_DIM
    qss, selbs, slopes, alibis = [], [], [], []
    for i in range(nh):
        qh = jnp.where(lane // ATT_HEAD_DIM == i, q, 0.0)
        q1, q2, _ = _split3(qh)
        gate = _dot_nt(k1, q1) + _dot_nt(k1, q2) + _dot_nt(k2, q1)
        selbs.append(jnp.where(_select_topk(gate, qb, 0), 0.0, NEG))
        qss.append((qh * scale).astype(BF16))
        slopes.append(sl_ref[0, :, i * ATT_HEAD_DIM:i * ATT_HEAD_DIM + 1])
        alibis.append(slopes[i] * rel)
    qs_all = jnp.concatenate(qss, axis=0)

    def logits(n):
        s_all = _dot_nt(kbf[n], qs_all)
        out = []
        for i in range(nh):
            rowterm = (jnp.sum(jnp.where(brow == n, selbs[i], 0.0), axis=0, keepdims=True)
                       + slopes[i] * ((n - qb) * MOBA_BLOCK).astype(F32))
            out.append(s_all[:, i * tq:(i + 1) * tq] + alibis[i] + rowterm)
        return out

    def update(i, blocks, carry):
        m, l, acc = carry
        m_new = m
        for _, s in blocks:
            m_new = jnp.maximum(m_new, jnp.max(s, axis=0, keepdims=True))
        alpha = jnp.exp(m - m_new)
        l = alpha * l
        acc = alpha * acc
        for n, s in blocks:
            p = jnp.exp(s - m_new)
            l = l + jnp.sum(p, axis=0, keepdims=True)
            acc = acc + _dot(vtb[n, pl.ds(i * ATT_HEAD_DIM, ATT_HEAD_DIM), :], p.astype(BF16))
        return m_new, l, acc

    def body(j, carry):
        n0 = 2 * j
        sa, sb = logits(n0), logits(n0 + 1)
        return tuple(update(i, [(n0, sa[i]), (n0 + 1, sb[i])], carry[i]) for i in range(nh))

    init = (jnp.full((1, tq), NEG, F32), jnp.zeros((1, tq), F32), jnp.zeros((ATT_HEAD_DIM, tq), F32))
    carry = lax.fori_loop(0, qb // 2, body, (init,) * nh)

    def finish(with_left):
        left = qb - 1
        s_left = logits(left) if with_left else None
        s_qk = _dot_nt(kbf[qb], qs_all)
        outs = []
        for i in range(nh):
            s_own = jnp.where(krow <= qcol, s_qk[:, i * tq:(i + 1) * tq] + alibis[i], NEG)
            blocks = ([(left, s_left[i])] if with_left else []) + [(qb, s_own)]
            m, l, acc = update(i, blocks, carry[i])
            outs.append(acc / l)
        return jnp.concatenate(outs, axis=0)

    o_t = lax.cond(qb % 2 == 1, lambda: finish(True), lambda: finish(False))
    o_ref[...] = (o_t.T * _silu(zc_ref[...])).astype(o_ref.dtype)


ATT_GROUP_HEADS = 4


def _alibi_slopes():
    return 2.0 ** (-8.0 * jnp.arange(1, ATT_HEADS + 1, dtype=F32) / ATT_HEADS)


def _attn_prompt(proj, bsz, T):
    tq = MOBA_BLOCK
    nq = T // tq
    nb = T // MOBA_BLOCK
    nbp = -(-nb // SUBLANES) * SUBLANES
    lw = ATT_GROUP_HEADS * ATT_HEAD_DIM
    hp_n = W_BRANCH // lw
    slopes = jnp.repeat(_alibi_slopes(), ATT_HEAD_DIM).reshape(hp_n, 1, lw)
    cpb = W_BRANCH // lw
    kv_t = lambda: pl.BlockSpec((1, lw, T), lambda b, hp, qi: (b, hp, 0))
    return pl.pallas_call(
        functools.partial(_attn_kernel, T=T, tq=tq),
        out_shape=(jax.ShapeDtypeStruct((bsz * T, W_BRANCH), BF16),
                   jax.ShapeDtypeStruct((bsz, W_BRANCH, T), F32),
                   jax.ShapeDtypeStruct((bsz, W_BRANCH, T), F32)),
        grid=(bsz, hp_n, nq),
        in_specs=[pl.BlockSpec((tq, lw), lambda b, hp, qi: (b * nq + qi, COL_Q * cpb + hp)),
                  pl.BlockSpec((T, lw), lambda b, hp, qi: (b, COL_K * cpb + hp)),
                  pl.BlockSpec((T, lw), lambda b, hp, qi: (b, COL_VA * cpb + hp)),
                  pl.BlockSpec((tq, lw), lambda b, hp, qi: (b * nq + qi, COL_ZC * cpb + hp)),
                  pl.BlockSpec((1, 1, lw), lambda b, hp, qi: (hp, 0, 0))],
        out_specs=(pl.BlockSpec((tq, lw), lambda b, hp, qi: (b * nq + qi, hp)), kv_t(), kv_t()),
        scratch_shapes=[pltpu.VMEM((nb, MOBA_BLOCK, lw), BF16), pltpu.VMEM((nb, lw, MOBA_BLOCK), BF16),
                        pltpu.VMEM((nbp, lw), F32)],
        compiler_params=_params("arbitrary", "arbitrary", "arbitrary"),
        name="attn_prompt",
    )(proj, proj, proj, proj, slopes)


def _gate_topk_kernel(q_ref, km_ref, o_ref):
    gate = jnp.sum(km_ref[0] * q_ref[...], axis=-1)
    idx = lax.broadcasted_iota(jnp.int32, gate.shape, 0).astype(F32)
    rid = lax.broadcasted_iota(jnp.int32, (SUBLANES, ATT_HEADS), 0)
    big = jnp.float32(1 << 30)
    out = jnp.zeros((SUBLANES, ATT_HEADS), F32)
    g = gate
    for r in range(MOBA_TOPK):
        m = jnp.max(g, axis=0, keepdims=True)
        first = jnp.min(jnp.where(g == m, idx, big), axis=0, keepdims=True)
        out = jnp.where(rid == r, jnp.broadcast_to(first, (SUBLANES, ATT_HEADS)), out)
        g = jnp.where(idx == first, -jnp.inf, g)
    o_ref[0] = out.astype(jnp.int32)


def _gate_topk(q3, kmean):
    bsz, nblk = kmean.shape[:2]
    return pl.pallas_call(
        _gate_topk_kernel,
        out_shape=jax.ShapeDtypeStruct((bsz, SUBLANES, ATT_HEADS), jnp.int32),
        grid=(bsz,),
        in_specs=[pl.BlockSpec((1, ATT_HEADS, ATT_HEAD_DIM), lambda b: (b, 0, 0)),
                  pl.BlockSpec((1, nblk, ATT_HEADS, ATT_HEAD_DIM), lambda b: (b, 0, 0, 0))],
        out_specs=pl.BlockSpec((1, SUBLANES, ATT_HEADS), lambda b: (b, 0, 0)),
        compiler_params=_params("arbitrary"),
        name="gate_topk",
    )(q3, kmean)


N_SEL_TILES = MOBA_TOPK * BLOCK_PAGES


def _attn_sample_kernel(pt_ref, idx_ref, q_ref, kn_ref, vn_ref, zc_ref, sl_ref, *refs, past_len):
    nt = SAMPLE_GROUP_HEADS * N_SEL_TILES
    k_refs = refs[:nt]
    v_refs = refs[nt:2 * nt]
    o_ref = refs[2 * nt]
    b = pl.program_id(0)
    g = pl.program_id(1)
    scale = ATT_HEAD_DIM ** -0.5
    lane = lax.broadcasted_iota(jnp.int32, (SUBLANES, PAGE_SIZE), 1)
    bf = lambda a: a.astype(BF16).astype(F32)
    for hh in range(SAMPLE_GROUP_HEADS):
        h = g * SAMPLE_GROUP_HEADS + hh
        q = jnp.broadcast_to(q_ref[0, hh] * scale, (SUBLANES, ATT_HEAD_DIM)).astype(BF16)
        slope = sl_ref[hh]
        logits = []
        for j in range(MOBA_TOPK):
            blk = idx_ref[(b * MOBA_TOPK + j) * ATT_HEADS + h]
            for half in range(BLOCK_PAGES):
                kpos = blk * MOBA_BLOCK + half * PAGE_SIZE + lane
                kt = k_refs[hh * N_SEL_TILES + j * BLOCK_PAGES + half][0, 0]
                s = _dot(q, kt.astype(BF16))
                logits.append(s - slope * (past_len - kpos).astype(F32))
        s_own = jnp.sum(q.astype(F32) * bf(kn_ref[0, hh]), axis=1, keepdims=True)
        m = s_own
        for s in logits:
            m = jnp.maximum(m, jnp.max(s, axis=1, keepdims=True))
        p_own = jnp.exp(s_own - m)
        l = p_own
        acc = bf(p_own) * bf(vn_ref[0, hh])
        for t, s in enumerate(logits):
            p = jnp.exp(s - m)
            l = l + jnp.sum(p, axis=1, keepdims=True)
            acc = acc + _dot_nt(p.astype(BF16), v_refs[hh * N_SEL_TILES + t][0, 0].astype(BF16))
        o = (acc / l) * _silu(zc_ref[0, hh])
        o_ref[0, hh] = o[0:1, :]


SAMPLE_GROUP_HEADS = 4


def _attn_sample(cache_kt, cache_vt, page_off, pt_flat, idx_flat, q4, kn4, vn4, zc4, n_pages):
    bsz = q4.shape[0]
    gh = SAMPLE_GROUP_HEADS
    past_len = n_pages * PAGE_SIZE
    slopes = jnp.broadcast_to(_alibi_slopes()[:, None, None], (ATT_HEADS, 1, PAGE_SIZE))

    def tile_spec(hh, j, half):
        def imap(b, g, pt, idx):
            h = g * gh + hh
            blk = idx[(b * MOBA_TOPK + j) * ATT_HEADS + h]
            return (page_off + pt[b * n_pages + blk * BLOCK_PAGES + half], h, 0, 0)
        return pl.BlockSpec((1, 1, ATT_HEAD_DIM, PAGE_SIZE), imap)

    tiles = [tile_spec(hh, j, half) for hh in range(gh) for j in range(MOBA_TOPK) for half in range(BLOCK_PAGES)]
    head = lambda: pl.BlockSpec((1, gh, 1, ATT_HEAD_DIM), lambda b, g, pt, idx: (b, g, 0, 0))
    return pl.pallas_call(
        functools.partial(_attn_sample_kernel, past_len=past_len),
        out_shape=jax.ShapeDtypeStruct((bsz, ATT_HEADS, 1, ATT_HEAD_DIM), F32),
        grid_spec=pltpu.PrefetchScalarGridSpec(
            num_scalar_prefetch=2,
            grid=(bsz, ATT_HEADS // gh),
            in_specs=[head(), head(), head(), head(),
                      pl.BlockSpec((gh, 1, PAGE_SIZE), lambda b, g, pt, idx: (g, 0, 0))] + tiles + tiles,
            out_specs=head()),
        compiler_params=_params("arbitrary", "arbitrary"),
        name="attn_sample",
    )(pt_flat, idx_flat, q4, kn4, vn4, zc4, slopes,
      *([cache_kt] * len(tiles)), *([cache_vt] * len(tiles)))


def _merge_kernel(oa_ref, ob_ref, oc_ref, ga_ref, gb_ref, gc_ref, w_ref, o_ref):
    acc = _sigmoid(ga_ref[...]) * _dot(oa_ref[...], w_ref[0])
    acc = acc + _sigmoid(gb_ref[...]) * _dot(ob_ref[...], w_ref[1])
    acc = acc + _sigmoid(gc_ref[...]) * _dot(oc_ref[...], w_ref[2])
    o_ref[...] = acc.astype(o_ref.dtype)


def _merge(o_a, o_b, o_c, proj, w_branch, tm, tn):
    m = o_a.shape[0]
    nt = D_MODEL // tn
    gate_blk0 = GATE_COLBLK * D_MODEL // tn
    br = lambda: pl.BlockSpec((tm, W_BRANCH), lambda j, i: (i, 0))
    gate = lambda r: pl.BlockSpec((tm, tn), lambda j, i: (i, gate_blk0 + r * nt + j))
    return pl.pallas_call(
        _merge_kernel,
        out_shape=jax.ShapeDtypeStruct((m, D_MODEL), BF16),
        grid=(nt, m // tm),
        in_specs=[br(), br(), br(), gate(0), gate(1), gate(2),
                  pl.BlockSpec((N_BRANCH, W_BRANCH, tn), lambda j, i: (0, 0, j))],
        out_specs=pl.BlockSpec((tm, tn), lambda j, i: (i, j)),
        compiler_params=_params("parallel", "parallel"),
        name="merge",
    )(o_a, o_b, o_c, proj, proj, proj, w_branch)


def _outproj_kernel(m_ref, w_ref, x_ref, g_ref, gn_ref, o_ref, h_ref):
    out = _dot(m_ref[...], w_ref[...])
    ms = jnp.mean(out * out, axis=-1, keepdims=True)
    y = x_ref[...] + out * lax.rsqrt(ms + EPS) * g_ref[...]
    o_ref[...] = y
    ms_y = jnp.mean(y * y, axis=-1, keepdims=True)
    h_ref[...] = (y * lax.rsqrt(ms_y + EPS) * gn_ref[...]).astype(h_ref.dtype)


def _outproj(merged, w_out, x, g_post, g_next, tm):
    m = merged.shape[0]
    rows = lambda: pl.BlockSpec((tm, D_MODEL), lambda i: (i, 0))
    gain = lambda: pl.BlockSpec((1, D_MODEL), lambda i: (0, 0))
    return pl.pallas_call(
        _outproj_kernel,
        out_shape=(jax.ShapeDtypeStruct((m, D_MODEL), F32), jax.ShapeDtypeStruct((m, D_MODEL), BF16)),
        grid=(m // tm,),
        in_specs=[rows(), pl.BlockSpec((D_MODEL, D_MODEL), lambda i: (0, 0)), rows(), gain(), gain()],
        out_specs=(rows(), rows()),
        compiler_params=_params("parallel"),
        name="outproj",
    )(merged, w_out, x, g_post.reshape(1, -1), g_next.reshape(1, -1))


def _seg(proj, blk, rows=None):
    out = proj[:, blk * W_BRANCH:(blk + 1) * W_BRANCH]
    return out if rows is None else out[:rows]


def _prompt_layer(x, h, bsz, T, lw, g_next, pool):
    (_, w_in, w_s, b_s, ln_g, ln_b, conv_w, conv_b, dt_bias, a_log, d_skip, g_ssm, w_br, w_out, g_post) = lw
    proj, kmean = _inproj(h, *w_in, 512, pool)
    o_a = _gmlp(proj, w_s, b_s, ln_g, ln_b, 512)
    L = min(SSM_CHUNK, T)
    nc = T // L
    prev8 = jnp.zeros((bsz, SUBLANES, CONV_DIM), F32)
    h0 = jnp.zeros((bsz, SSM_HEADS * SSM_HEAD_DIM, SSM_STATE), F32)
    o_b, h_fin = _ssd(proj, XD_COLBLK, proj, COL_ZB, prev8, h0, conv_w, conv_b, dt_bias, a_log, d_skip, g_ssm,
                      bsz, nc, L, T)
    o_c, k_t, v_t = _attn_prompt(proj, bsz, T)
    merged = _merge(o_a, o_b, o_c, proj, w_br, 512, 1024)
    y, h_next = _outproj(merged, w_out, x, g_post, g_next, 256)
    k_new = k_t.reshape(bsz, ATT_HEADS, ATT_HEAD_DIM, T).transpose(0, 3, 1, 2)
    v_new = v_t.reshape(bsz, ATT_HEADS, ATT_HEAD_DIM, T).transpose(0, 3, 1, 2)
    xbc_raw = proj[:, XD_COLBLK * XD_WIDTH:XD_COLBLK * XD_WIDTH + CONV_DIM].reshape(bsz, T, CONV_DIM)
    conv_new = xbc_raw[:, T - (SSM_CONV - 1):]
    ssm_new = h_fin.reshape(bsz, SSM_HEADS, SSM_HEAD_DIM, SSM_STATE)
    return y, h_next, k_new, v_new, ssm_new, conv_new, kmean


SAMPLE_ROWS = 16
SAMPLE_L = 128


def _sample_layer(x16, h, bsz, conv_prev, ssm_prev, cache_kt, cache_vt, kmean, page_off, pt_flat, n_pages, lw,
                  g_next):
    (_, w_in, w_s, b_s, ln_g, ln_b, conv_w, conv_b, dt_bias, a_log, d_skip, g_ssm, w_br, w_out, g_post) = lw
    proj = _inproj(h, *w_in, SAMPLE_ROWS)
    o_a, v_a = _gmlp_single(proj, w_s, b_s, ln_g, ln_b)
    xd = proj[:bsz, XD_COLBLK * XD_WIDTH:]
    xd_rows = jnp.zeros((bsz, SAMPLE_L, XD_WIDTH), F32).at[:, 0].set(xd).reshape(bsz * SAMPLE_L, XD_WIDTH)
    zb_rows = jnp.zeros((bsz, SAMPLE_L, W_BRANCH), F32).at[:, 0].set(_seg(proj, COL_ZB, bsz))
    zb_rows = zb_rows.reshape(bsz * SAMPLE_L, W_BRANCH)
    prev8 = jnp.concatenate([jnp.zeros((bsz, SUBLANES - (SSM_CONV - 1), CONV_DIM), F32), conv_prev], axis=1)
    h0 = ssm_prev.reshape(bsz, SSM_HEADS * SSM_HEAD_DIM, SSM_STATE)
    ob_rows, h_fin = _ssd(xd_rows, 0, zb_rows, 0, prev8, h0, conv_w, conv_b, dt_bias, a_log, d_skip, g_ssm,
                          bsz, 1, SAMPLE_L, 1)
    o_b = ob_rows.reshape(bsz, SAMPLE_L, W_BRANCH)[:, 0]
    o_b = jnp.pad(o_b, ((0, SAMPLE_ROWS - bsz), (0, 0)))
    conv_new = jnp.concatenate([conv_prev[:, 1:], xd[:, None, :CONV_DIM]], axis=1)
    ssm_new = h_fin.reshape(bsz, SSM_HEADS, SSM_HEAD_DIM, SSM_STATE)
    q = _seg(proj, COL_Q, bsz)
    k_new = _seg(proj, COL_K, bsz)
    v_new = _seg(proj, COL_VA, bsz)
    top = _gate_topk(q.reshape(bsz, ATT_HEADS, ATT_HEAD_DIM), kmean)
    idx_flat = top[:, :MOBA_TOPK].reshape(-1)
    heads4 = lambda a: a.reshape(bsz, ATT_HEADS, 1, ATT_HEAD_DIM)
    att = _attn_sample(cache_kt, cache_vt, page_off, pt_flat, idx_flat, heads4(q), heads4(k_new), heads4(v_new),
                       heads4(_seg(proj, COL_ZC, bsz)), n_pages)
    o_c = jnp.pad(att.reshape(bsz, W_BRANCH), ((0, SAMPLE_ROWS - bsz), (0, 0))).astype(BF16)
    merged = _merge(o_a, o_b, o_c, proj, w_br, SAMPLE_ROWS, 1024)
    y, h_next = _outproj(merged, w_out, x16, g_post, g_next, SAMPLE_ROWS)
    shape4 = (bsz, 1, ATT_HEADS, ATT_HEAD_DIM)
    return (y, h_next, k_new.reshape(shape4), v_new.reshape(shape4), ssm_new, conv_new,
            v_a[:bsz].reshape(bsz, 1, W_BRANCH))


def kernel(x_prompt, x_sample, cache_k, cache_v, state_ssm, state_conv, page_table, g_pre, w_in, w_spatial,
           b_spatial, ln_v_g, ln_v_b, conv_w, conv_b, dt_bias, a_log, d_skip, g_ssm, w_branch, w_out, g_post):
    depth = w_in.shape[0]
    bp, T, _ = x_prompt.shape
    bs = x_sample.shape[0]
    n_phys = cache_k.shape[1]
    n_pages = page_table.shape[1]
    pt_flat = page_table.reshape(-1).astype(jnp.int32)
    xp = x_prompt.reshape(bp * T, D_MODEL)
    xs = jnp.pad(x_sample.reshape(bs, D_MODEL), ((0, SAMPLE_ROWS - bs), (0, 0)))
    to_t = lambda c: c.transpose(0, 1, 3, 4, 2).reshape(depth * n_phys, ATT_HEADS, ATT_HEAD_DIM, PAGE_SIZE)
    cache_kt, cache_vt = to_t(cache_k), to_t(cache_v)
    w_in_t = w_in.transpose(0, 2, 1)
    outs = [[] for _ in range(9)]
    hp = _prenorm(xp, g_pre[0], 256)
    hs = _prenorm(xs, g_pre[0], SAMPLE_ROWS)
    for l in range(depth):
        lw = (g_pre[l], (w_in_t, l), w_spatial[l], b_spatial[l], ln_v_g[l], ln_v_b[l], conv_w[l],
              conv_b[l], dt_bias[l], a_log[l], d_skip[l], g_ssm[l], w_branch[l].astype(BF16),
              w_out[l].astype(BF16), g_post[l])
        g_next = g_pre[(l + 1) % depth]
        pool = (cache_kt, l * n_phys, pt_flat, bs, n_pages)
        xp, hp, kp, vp, sp, cp, kmean = _prompt_layer(xp, hp, bp, T, lw, g_next, pool)
        xs, hs, ks, vs, ss, cs, gv = _sample_layer(xs, hs, bs, state_conv[l], state_ssm[l], cache_kt, cache_vt,
                                                   kmean, l * n_phys, pt_flat, n_pages, lw, g_next)
        for lst, val in zip(outs, (kp, vp, sp, cp, ks, vs, ss, cs, gv)):
            lst.append(val)
    return (xp.reshape(bp, T, D_MODEL), xs[:bs].reshape(bs, 1, D_MODEL)) + tuple(jnp.stack(o) for o in outs)
```

```python
import functools

import jax
import jax.numpy as jnp
import numpy as np
from jax import lax
from jax.experimental import pallas as pl
from jax.experimental.pallas import tpu as pltpu

F32 = jnp.float32
BF16 = jnp.bfloat16

D_MODEL = 2048
W_BRANCH = 1024
N_BRANCH = 3
GMLP_GROUPS = 4
GMLP_CHUNK = 128
GMLP_GW = W_BRANCH // GMLP_GROUPS
SSM_HEAD_DIM = 64
SSM_HEADS = 16
SSM_GROUPS = 2
SSM_STATE = 128
SSM_CONV = 4
SSM_CHUNK = 256
CONV_DIM = W_BRANCH + 2 * SSM_GROUPS * SSM_STATE
ATT_HEAD_DIM = 64
ATT_HEADS = 16
MOBA_BLOCK = 256
MOBA_TOPK = 3
PAGE_SIZE = 128
EPS = 1e-6
IN_SIZES = [W_BRANCH, W_BRANCH, W_BRANCH, CONV_DIM, W_BRANCH, SSM_HEADS,
            W_BRANCH, W_BRANCH, W_BRANCH, W_BRANCH, N_BRANCH * D_MODEL]

LANES = 128
SUBLANES = 8
VMEM_LIMIT = 56 * 1024 * 1024

COL_U, COL_V, COL_ZA, COL_ZB, COL_Q, COL_K, COL_VA, COL_ZC = range(8)
GATE_COLBLK = 4
XD_COLBLK = 7
XD_WIDTH = D_MODEL
N_PROJ = 8 * W_BRANCH + N_BRANCH * D_MODEL + XD_WIDTH
NEG = -1e30


def _params(*sem):
    return pltpu.CompilerParams(dimension_semantics=sem, vmem_limit_bytes=VMEM_LIMIT)


def _sigmoid(x):
    return 1.0 / (1.0 + jnp.exp(-x))


def _silu(x):
    return x * _sigmoid(x)


def _gelu(x):
    return 0.5 * x * (1.0 + jnp.tanh(0.7978845608028654 * (x + 0.044715 * (x * x * x))))


def _softplus(x):
    return jnp.maximum(x, 0.0) + jnp.log1p(jnp.exp(-jnp.abs(x)))


def _dot(a, b):
    return jnp.dot(a, b, preferred_element_type=F32)


def _dot_nt(a, b):
    return lax.dot_general(a, b, (((1,), (1,)), ((), ())), preferred_element_type=F32)


def _split3(x):
    x1 = x.astype(BF16)
    r1 = x - x1.astype(F32)
    x2 = r1.astype(BF16)
    r2 = r1 - x2.astype(F32)
    return x1, x2, r2.astype(BF16)


def _prenorm_kernel(x_ref, g_ref, o_ref):
    x = x_ref[...]
    ms = jnp.mean(x * x, axis=-1, keepdims=True)
    o_ref[...] = (x * lax.rsqrt(ms + EPS) * g_ref[...]).astype(o_ref.dtype)


def _prenorm(x, g, tm):
    m, d = x.shape
    return pl.pallas_call(
        _prenorm_kernel,
        out_shape=jax.ShapeDtypeStruct((m, d), BF16),
        grid=(m // tm,),
        in_specs=[pl.BlockSpec((tm, d), lambda i: (i, 0)),
                  pl.BlockSpec((1, d), lambda i: (0, 0))],
        out_specs=pl.BlockSpec((tm, d), lambda i: (i, 0)),
        compiler_params=_params("parallel"),
        name="prenorm",
    )(x, g.reshape(1, d))


W_HALF = 512
W_ROW_ALIGN = 16
PROJ_TN = 2 * W_HALF
BLOCK_PAGES = MOBA_BLOCK // PAGE_SIZE


def _w_in_row_offsets():
    start = dict(zip("u v za xbc zb dt q k va zc gl".split(), [0] + [int(c) for c in np.cumsum(IN_SIZES)[:-1]]))
    offs = []
    for name in "u v za zb q k va zc".split():
        offs += [start[name] + t * W_HALF for t in range(W_BRANCH // W_HALF)]
    offs += [start["gl"] + t * W_HALF for t in range(N_BRANCH * D_MODEL // W_HALF)]
    offs += [start["xbc"] + t * W_HALF for t in range(CONV_DIM // W_HALF)]
    offs += [start["dt"]]
    assert len(offs) * W_HALF == N_PROJ and max(offs) + W_HALF <= sum(IN_SIZES)
    assert all(o % W_ROW_ALIGN == 0 for o in offs)
    return jnp.asarray([o // W_ROW_ALIGN for o in offs], jnp.int32)


def _block_mean_keys(page_refs, km_ref):
    for i in range(len(page_refs) // BLOCK_PAGES):
        s = page_refs[i * BLOCK_PAGES][0]
        for j in range(1, BLOCK_PAGES):
            s = s + page_refs[i * BLOCK_PAGES + j][0]
        km_ref[0, i] = jnp.sum(s, axis=-1) * (1.0 / MOBA_BLOCK)


def _inproj_kernel(offs_ref, x_ref, w0_ref, w1_ref, o_ref, wbf):
    @pl.when(pl.program_id(1) == 0)
    def _():
        wbf[0:W_HALF, :] = w0_ref[...].astype(BF16)
        wbf[W_HALF:, :] = w1_ref[...].astype(BF16)

    o_ref[...] = _dot_nt(x_ref[...], wbf[...])


def _inproj(h, w_t_all, layer, tm):
    m, k = h.shape
    half = lambda t: pl.BlockSpec((pl.Squeezed(), pl.Element(W_HALF), pl.Element(k)),
                                  lambda j, i, offs: (layer, offs[2 * j + t] * W_ROW_ALIGN, 0))
    return pl.pallas_call(
        _inproj_kernel,
        out_shape=jax.ShapeDtypeStruct((m, N_PROJ), F32),
        grid_spec=pltpu.PrefetchScalarGridSpec(
            num_scalar_prefetch=1,
            grid=(N_PROJ // PROJ_TN, m // tm),
            in_specs=[pl.BlockSpec((tm, k), lambda j, i, offs: (i, 0)), half(0), half(1)],
            out_specs=pl.BlockSpec((tm, PROJ_TN), lambda j, i, offs: (i, j)),
            scratch_shapes=[pltpu.VMEM((PROJ_TN, k), BF16)]),
        compiler_params=_params("arbitrary", "arbitrary"),
        name="inproj",
    )(_w_in_row_offsets(), h, w_t_all, w_t_all)


def _layernorm_rows(x, g, b):
    mu = jnp.mean(x, axis=-1, keepdims=True)
    d = x - mu
    var = jnp.mean(d * d, axis=-1, keepdims=True)
    return d * lax.rsqrt(var + EPS) * g + b


def _gmlp_kernel(u_ref, v_ref, z_ref, ws_ref, bt_ref, lg_ref, lb_ref, o_ref, *, cc, nch):
    row = lax.broadcasted_iota(jnp.int32, (cc, cc), 0)
    col = lax.broadcasted_iota(jnp.int32, (cc, cc), 1)
    tri = row >= col
    for c in range(nch):
        sl = pl.ds(c * cc, cc)
        va = _layernorm_rows(_gelu(v_ref[sl, :]), lg_ref[...], lb_ref[...])
        for g in range(GMLP_GROUPS):
            gs = slice(g * GMLP_GW, (g + 1) * GMLP_GW)
            wc = jnp.where(tri, ws_ref[g], 0.0).astype(BF16)
            mixed = _dot(wc, va[:, gs].astype(BF16)) + bt_ref[:, g:g + 1]
            o = _gelu(u_ref[sl, gs]) * mixed * _silu(z_ref[sl, gs])
            o_ref[sl, gs] = o.astype(o_ref.dtype)


def _gmlp(proj, w_s, b_s, ln_g, ln_b, tr):
    m = proj.shape[0]
    cc = GMLP_CHUNK
    return pl.pallas_call(
        functools.partial(_gmlp_kernel, cc=cc, nch=tr // cc),
        out_shape=jax.ShapeDtypeStruct((m, W_BRANCH), BF16),
        grid=(m // tr,),
        in_specs=[pl.BlockSpec((tr, W_BRANCH), lambda i: (i, COL_U)),
                  pl.BlockSpec((tr, W_BRANCH), lambda i: (i, COL_V)),
                  pl.BlockSpec((tr, W_BRANCH), lambda i: (i, COL_ZA)),
                  pl.BlockSpec((GMLP_GROUPS, cc, cc), lambda i: (0, 0, 0)),
                  pl.BlockSpec((cc, GMLP_GROUPS), lambda i: (0, 0)),
                  pl.BlockSpec((1, W_BRANCH), lambda i: (0, 0)),
                  pl.BlockSpec((1, W_BRANCH), lambda i: (0, 0))],
        out_specs=pl.BlockSpec((tr, W_BRANCH), lambda i: (i, 0)),
        compiler_params=_params("parallel"),
        name="gmlp",
    )(proj, proj, proj, w_s, b_s.T, ln_g.reshape(1, -1), ln_b.reshape(1, -1))


def _gmlp1_kernel(u_ref, v_ref, z_ref, w0_ref, b0_ref, lg_ref, lb_ref, o_ref, va_ref):
    va = _layernorm_rows(_gelu(v_ref[...]), lg_ref[...], lb_ref[...])
    va_ref[...] = va
    mixed = w0_ref[...] * va + b0_ref[...]
    o_ref[...] = (_gelu(u_ref[...]) * mixed * _silu(z_ref[...])).astype(o_ref.dtype)


def _gmlp_single(proj, w_s, b_s, ln_g, ln_b):
    m = proj.shape[0]
    w0 = jnp.repeat(w_s[:, 0, 0], GMLP_GW).reshape(1, W_BRANCH)
    b0 = jnp.repeat(b_s[:, 0], GMLP_GW).reshape(1, W_BRANCH)
    row = lambda: pl.BlockSpec((1, W_BRANCH), lambda i: (0, 0))
    return pl.pallas_call(
        _gmlp1_kernel,
        out_shape=(jax.ShapeDtypeStruct((m, W_BRANCH), BF16),
                   jax.ShapeDtypeStruct((m, W_BRANCH), F32)),
        grid=(1,),
        in_specs=[pl.BlockSpec((m, W_BRANCH), lambda i: (0, COL_U)),
                  pl.BlockSpec((m, W_BRANCH), lambda i: (0, COL_V)),
                  pl.BlockSpec((m, W_BRANCH), lambda i: (0, COL_ZA)),
                  row(), row(), row(), row()],
        out_specs=(pl.BlockSpec((m, W_BRANCH), lambda i: (0, 0)),
                   pl.BlockSpec((m, W_BRANCH), lambda i: (0, 0))),
        compiler_params=_params("arbitrary"),
        name="gmlp_single",
    )(proj, proj, proj, w0, b0, ln_g.reshape(1, -1), ln_b.reshape(1, -1))


def _ssd_kernel(xd_ref, zb_ref, prev_ref, h0_ref, cw_ref, cb_ref, dtb_ref, alog_ref, dsk_ref, gs_ref,
                ob_ref, hfin_ref, state_scr, tail_scr, y_scr, xs_scr, *, L, t_valid, nc):
    c = pl.program_id(1)

    @pl.when(c == 0)
    def _():
        state_scr[...] = h0_ref[0]
        tail_scr[...] = prev_ref[0]

    raw = xd_ref[:, :CONV_DIM]
    tail = tail_scr[...]
    rid = lax.broadcasted_iota(jnp.int32, (SUBLANES, CONV_DIM), 0)
    acc = cb_ref[...] + raw * cw_ref[SSM_CONV - 1:SSM_CONV, :]
    for k in range(1, SSM_CONV):
        rolled = pltpu.roll(raw, k, axis=0)
        head = jnp.where(rid < k, pltpu.roll(tail, k, axis=0), rolled[:SUBLANES])
        shifted = jnp.concatenate([head, rolled[SUBLANES:]], axis=0)
        acc = acc + shifted * cw_ref[SSM_CONV - 1 - k:SSM_CONV - k, :]
    tail_scr[...] = raw[L - SUBLANES:, :]
    xbc = _silu(acc)
    x = xbc[:, :W_BRANCH]
    gn = SSM_GROUPS * SSM_STATE
    bm = xbc[:, W_BRANCH:W_BRANCH + gn].astype(BF16)
    cm = xbc[:, W_BRANCH + gn:].astype(BF16)

    dt = _softplus(xd_ref[:, CONV_DIM:CONV_DIM + LANES] + dtb_ref[...])
    if t_valid < nc * L:
        grow = c * L + lax.broadcasted_iota(jnp.int32, (L, LANES), 0)
        dt = jnp.where(grow < t_valid, dt, 0.0)
    da = dt * (-jnp.exp(alog_ref[...]))
    row = lax.broadcasted_iota(jnp.int32, (L, L), 0)
    col = lax.broadcasted_iota(jnp.int32, (L, L), 1)
    causal = row >= col
    tril = jnp.where(causal, 1.0, 0.0).astype(BF16)
    d1, d2, d3 = _split3(da)
    a_cs = _dot(tril, d1) + _dot(tril, d2) + _dot(tril, d3)
    a_cs_t = a_cs.T
    dt_t = dt.T
    hg = SSM_HEADS // SSM_GROUPS
    gw = hg * SSM_HEAD_DIM
    for g in range(SSM_GROUPS):
        bg = bm[:, g * SSM_STATE:(g + 1) * SSM_STATE]
        cg = cm[:, g * SSM_STATE:(g + 1) * SSM_STATE]
        cb = _dot_nt(cg, bg)
        st_g = state_scr[g * gw:(g + 1) * gw, :]
        y_off = _dot_nt(cg, st_g.astype(BF16))
        for hh in range(hg):
            h = g * hg + hh
            hs = slice(h * SSM_HEAD_DIM, (h + 1) * SSM_HEAD_DIM)
            ac = a_cs[:, h:h + 1]
            ar = a_cs_t[h:h + 1, :]
            seg = jnp.exp(jnp.where(causal, ac - ar, -jnp.inf))
            wm = (cb * seg * dt_t[h:h + 1, :]).astype(BF16)
            xh = x[:, hs]
            yd = _dot(wm, xh.astype(BF16))
            yo = y_off[:, hh * SSM_HEAD_DIM:(hh + 1) * SSM_HEAD_DIM] * jnp.exp(ac)
            y_scr[:, hs] = yd + yo
            a_last = a_cs[L - 1:L, h:h + 1]
            xs_scr[:, hs] = xh * (jnp.exp(a_last - ac) * dt[:, h:h + 1])
        sts = _dot(xs_scr[:, g * gw:(g + 1) * gw].T.astype(BF16), bg)
        for hh in range(hg):
            h = g * hg + hh
            rs = slice(h * SSM_HEAD_DIM, (h + 1) * SSM_HEAD_DIM)
            a_last = a_cs[L - 1:L, h:h + 1]
            state_scr[rs, :] = (jnp.exp(a_last) * state_scr[rs, :]
                                + sts[hh * SSM_HEAD_DIM:(hh + 1) * SSM_HEAD_DIM, :])

    gated = (y_scr[...] + dsk_ref[...] * x) * _silu(zb_ref[...])
    ms = jnp.mean(gated * gated, axis=-1, keepdims=True)
    ob_ref[...] = (gated * lax.rsqrt(ms + EPS) * gs_ref[...]).astype(ob_ref.dtype)

    @pl.when(c == nc - 1)
    def _():
        hfin_ref[0] = state_scr[...]


def _ssd(xd_arr, xd_colblk, zb_arr, zb_colblk, prev8, h0, conv_w, conv_b, dt_bias, a_log, d_skip, g_ssm,
         bsz, nc, L, t_valid):
    rows = bsz * nc * L
    pad16 = lambda a: jnp.pad(a, (0, LANES - SSM_HEADS)).reshape(1, LANES)
    const = lambda shape: pl.BlockSpec(shape, lambda b, c: (0,) * len(shape))
    return pl.pallas_call(
        functools.partial(_ssd_kernel, L=L, t_valid=t_valid, nc=nc),
        out_shape=(jax.ShapeDtypeStruct((rows, W_BRANCH), BF16),
                   jax.ShapeDtypeStruct((bsz, SSM_HEADS * SSM_HEAD_DIM, SSM_STATE), F32)),
        grid=(bsz, nc),
        in_specs=[pl.BlockSpec((L, XD_WIDTH), lambda b, c: (b * nc + c, xd_colblk)),
                  pl.BlockSpec((L, W_BRANCH), lambda b, c: (b * nc + c, zb_colblk)),
                  pl.BlockSpec((1, SUBLANES, CONV_DIM), lambda b, c: (b, 0, 0)),
                  pl.BlockSpec((1, SSM_HEADS * SSM_HEAD_DIM, SSM_STATE), lambda b, c: (b, 0, 0)),
                  const((SSM_CONV, CONV_DIM)), const((1, CONV_DIM)), const((1, LANES)), const((1, LANES)),
                  const((1, W_BRANCH)), const((1, W_BRANCH))],
        out_specs=(pl.BlockSpec((L, W_BRANCH), lambda b, c: (b * nc + c, 0)),
                   pl.BlockSpec((1, SSM_HEADS * SSM_HEAD_DIM, SSM_STATE), lambda b, c: (b, 0, 0))),
        scratch_shapes=[pltpu.VMEM((SSM_HEADS * SSM_HEAD_DIM, SSM_STATE), F32),
                        pltpu.VMEM((SUBLANES, CONV_DIM), F32),
                        pltpu.VMEM((L, W_BRANCH), F32),
                        pltpu.VMEM((L, W_BRANCH), F32)],
        compiler_params=_params("arbitrary", "arbitrary"),
        name="ssd",
    )(xd_arr, zb_arr, prev8, h0, conv_w, conv_b.reshape(1, -1), pad16(dt_bias), pad16(a_log),
      jnp.repeat(d_skip, SSM_HEAD_DIM).reshape(1, -1), g_ssm.reshape(1, -1))


def _select_topk(gate, n_elig, axis):
    idx = lax.broadcasted_iota(jnp.int32, gate.shape, axis).astype(F32)
    big = jnp.float32(1 << 30)
    g = jnp.where(idx < n_elig.astype(F32), gate, -jnp.inf)
    sel = jnp.zeros(gate.shape, jnp.bool_)
    for _ in range(MOBA_TOPK):
        m = jnp.max(g, axis=axis, keepdims=True)
        hit = (g == m) & (m > -jnp.inf) & (m < jnp.inf)
        first = jnp.min(jnp.where(hit, idx, big), axis=axis, keepdims=True)
        pick = idx == first
        sel = sel | pick
        g = jnp.where(pick, -jnp.inf, g)
    return sel


def _attn_kernel(pt_ref, q_ref, k_ref, v_ref, zc_ref, sl_ref, *refs, T, tq):
    n_pg = len(refs) - 7
    page_refs = refs[:n_pg]
    o_ref, kt_ref, vt_ref, pmean_ref, kbf, vtb, km_scr = refs[n_pg:]
    _block_mean_keys(page_refs, pmean_ref)
    nb = T // MOBA_BLOCK
    nbp = km_scr.shape[0]
    qb = pl.program_id(2)

    @pl.when(qb == 0)
    def _():
        km_scr[...] = jnp.zeros(km_scr.shape, F32)
        for n in range(nb):
            rs = slice(n * MOBA_BLOCK, (n + 1) * MOBA_BLOCK)
            k = k_ref[rs, :]
            vt = v_ref[rs, :].T
            kt_ref[0, :, rs] = k.T
            vt_ref[0, :, rs] = vt
            kbf[n] = k.astype(BF16)
            vtb[n] = vt.astype(BF16)
            km_scr[n:n + 1, :] = jnp.mean(k, axis=0, keepdims=True)

    q = q_ref[...]
    lw = q.shape[1]
    lane = lax.broadcasted_iota(jnp.int32, (tq, lw), 1)
    krow = lax.broadcasted_iota(jnp.int32, (MOBA_BLOCK, tq), 0)
    qcol = lax.broadcasted_iota(jnp.int32, (MOBA_BLOCK, tq), 1)
    rel = (krow - qcol).astype(F32)
    brow = lax.broadcasted_iota(jnp.int32, (nbp, tq), 0)
    scale = ATT_HEAD_DIM ** -0.5
    k1, k2, _ = _split3(km_scr[...])
    nh = lw // ATT_HEAD_DIM
    qss, selbs, slopes, alibis = [], [], [], []
    for i in range(nh):
        qh = jnp.where(lane // ATT_HEAD_DIM == i, q, 0.0)
        q1, q2, _ = _split3(qh)
        gate = _dot_nt(k1, q1) + _dot_nt(k1, q2) + _dot_nt(k2, q1)
        selbs.append(jnp.where(_select_topk(gate, qb, 0), 0.0, NEG))
        qss.append((qh * scale).astype(BF16))
        slopes.append(sl_ref[0, :, i * ATT_HEAD_DIM:i * ATT_HEAD_DIM + 1])
        alibis.append(slopes[i] * rel)
    qs_all = jnp.concatenate(qss, axis=0)

    def logits(n):
        s_all = _dot_nt(kbf[n], qs_all)
        out = []
        for i in range(nh):
            rowterm = (jnp.sum(jnp.where(brow == n, selbs[i], 0.0), axis=0, keepdims=True)
                       + slopes[i] * ((n - qb) * MOBA_BLOCK).astype(F32))
            out.append(s_all[:, i * tq:(i + 1) * tq] + alibis[i] + rowterm)
        return out

    def update(i, blocks, carry):
        m, l, acc = carry
        m_new = m
        for _, s in blocks:
            m_new = jnp.maximum(m_new, jnp.max(s, axis=0, keepdims=True))
        alpha = jnp.exp(m - m_new)
        l = alpha * l
        acc = alpha * acc
        for n, s in blocks:
            p = jnp.exp(s - m_new)
            l = l + jnp.sum(p, axis=0, keepdims=True)
            acc = acc + _dot(vtb[n, pl.ds(i * ATT_HEAD_DIM, ATT_HEAD_DIM), :], p.astype(BF16))
        return m_new, l, acc

    def body(j, carry):
        n0 = 2 * j
        sa, sb = logits(n0), logits(n0 + 1)
        return tuple(update(i, [(n0, sa[i]), (n0 + 1, sb[i])], carry[i]) for i in range(nh))

    init = (jnp.full((1, tq), NEG, F32), jnp.zeros((1, tq), F32), jnp.zeros((ATT_HEAD_DIM, tq), F32))
    carry = lax.fori_loop(0, qb // 2, body, (init,) * nh)

    def finish(with_left):
        left = qb - 1
        s_left = logits(left) if with_left else None
        s_qk = _dot_nt(kbf[qb], qs_all)
        outs = []
        for i in range(nh):
            s_own = jnp.where(krow <= qcol, s_qk[:, i * tq:(i + 1) * tq] + alibis[i], NEG)
            blocks = ([(left, s_left[i])] if with_left else []) + [(qb, s_own)]
            m, l, acc = update(i, blocks, carry[i])
            outs.append(acc / l)
        return jnp.concatenate(outs, axis=0)

    o_t = lax.cond(qb % 2 == 1, lambda: finish(True), lambda: finish(False))
    o_ref[...] = (o_t.T * _silu(zc_ref[...])).astype(o_ref.dtype)


ATT_GROUP_HEADS = 4


def _alibi_slopes():
    return 2.0 ** (-8.0 * jnp.arange(1, ATT_HEADS + 1, dtype=F32) / ATT_HEADS)


def _attn_prompt(proj, bsz, T, pool):
    tq = MOBA_BLOCK
    nq = T // tq
    nb = T // MOBA_BLOCK
    nbp = -(-nb // SUBLANES) * SUBLANES
    lw = ATT_GROUP_HEADS * ATT_HEAD_DIM
    hp_n = W_BRANCH // lw
    slopes = jnp.repeat(_alibi_slopes(), ATT_HEAD_DIM).reshape(hp_n, 1, lw)
    cpb = W_BRANCH // lw
    cache_t, page_off, pt_flat, n_seq, n_pages = pool
    steps = bsz * hp_n * nq
    pps = n_seq * n_pages // steps
    bps = pps // BLOCK_PAGES
    assert pps * steps == n_seq * n_pages and bps * BLOCK_PAGES == pps and n_pages % pps == 0
    spb = n_pages // pps
    step = lambda b, hp, qi: (b * hp_n + hp) * nq + qi

    def page_spec(t):
        return pl.BlockSpec((1, ATT_HEADS, ATT_HEAD_DIM, PAGE_SIZE),
                            lambda b, hp, qi, pt: (page_off + pt[step(b, hp, qi) * pps + t], 0, 0, 0))

    kv_t = lambda: pl.BlockSpec((1, lw, T), lambda b, hp, qi, pt: (b, hp, 0))
    return pl.pallas_call(
        functools.partial(_attn_kernel, T=T, tq=tq),
        out_shape=(jax.ShapeDtypeStruct((bsz * T, W_BRANCH), BF16),
                   jax.ShapeDtypeStruct((bsz, W_BRANCH, T), F32),
                   jax.ShapeDtypeStruct((bsz, W_BRANCH, T), F32),
                   jax.ShapeDtypeStruct((n_seq, n_pages // BLOCK_PAGES, ATT_HEADS, ATT_HEAD_DIM), F32)),
        grid_spec=pltpu.PrefetchScalarGridSpec(
            num_scalar_prefetch=1,
            grid=(bsz, hp_n, nq),
            in_specs=[pl.BlockSpec((tq, lw), lambda b, hp, qi, pt: (b * nq + qi, COL_Q * cpb + hp)),
                      pl.BlockSpec((T, lw), lambda b, hp, qi, pt: (b, COL_K * cpb + hp)),
                      pl.BlockSpec((T, lw), lambda b, hp, qi, pt: (b, COL_VA * cpb + hp)),
                      pl.BlockSpec((tq, lw), lambda b, hp, qi, pt: (b * nq + qi, COL_ZC * cpb + hp)),
                      pl.BlockSpec((1, 1, lw), lambda b, hp, qi, pt: (hp, 0, 0))]
                     + [page_spec(t) for t in range(pps)],
            out_specs=(pl.BlockSpec((tq, lw), lambda b, hp, qi, pt: (b * nq + qi, hp)), kv_t(), kv_t(),
                       pl.BlockSpec((1, bps, ATT_HEADS, ATT_HEAD_DIM),
                                    lambda b, hp, qi, pt: (step(b, hp, qi) // spb, step(b, hp, qi) % spb, 0, 0))),
            scratch_shapes=[pltpu.VMEM((nb, MOBA_BLOCK, lw), BF16), pltpu.VMEM((nb, lw, MOBA_BLOCK), BF16),
                            pltpu.VMEM((nbp, lw), F32)]),
        compiler_params=_params("arbitrary", "arbitrary", "arbitrary"),
        name="attn_prompt",
    )(pt_flat, proj, proj, proj, proj, slopes, *([cache_t] * pps))


def _gate_topk_kernel(q_ref, km_ref, o_ref):
    gate = jnp.sum(km_ref[0] * q_ref[...], axis=-1)
    idx = lax.broadcasted_iota(jnp.int32, gate.shape, 0).astype(F32)
    rid = lax.broadcasted_iota(jnp.int32, (SUBLANES, ATT_HEADS), 0)
    big = jnp.float32(1 << 30)
    out = jnp.zeros((SUBLANES, ATT_HEADS), F32)
    g = gate
    for r in range(MOBA_TOPK):
        m = jnp.max(g, axis=0, keepdims=True)
        first = jnp.min(jnp.where(g == m, idx, big), axis=0, keepdims=True)
        out = jnp.where(rid == r, jnp.broadcast_to(first, (SUBLANES, ATT_HEADS)), out)
        g = jnp.where(idx == first, -jnp.inf, g)
    o_ref[0] = out.astype(jnp.int32)


def _gate_topk(q3, kmean):
    bsz, nblk = kmean.shape[:2]
    return pl.pallas_call(
        _gate_topk_kernel,
        out_shape=jax.ShapeDtypeStruct((bsz, SUBLANES, ATT_HEADS), jnp.int32),
        grid=(bsz,),
        in_specs=[pl.BlockSpec((1, ATT_HEADS, ATT_HEAD_DIM), lambda b: (b, 0, 0)),
                  pl.BlockSpec((1, nblk, ATT_HEADS, ATT_HEAD_DIM), lambda b: (b, 0, 0, 0))],
        out_specs=pl.BlockSpec((1, SUBLANES, ATT_HEADS), lambda b: (b, 0, 0)),
        compiler_params=_params("arbitrary"),
        name="gate_topk",
    )(q3, kmean)


N_SEL_TILES = MOBA_TOPK * BLOCK_PAGES


def _attn_sample_kernel(pt_ref, idx_ref, q_ref, kn_ref, vn_ref, zc_ref, sl_ref, *refs, past_len):
    nt = SAMPLE_GROUP_HEADS * N_SEL_TILES
    k_refs = refs[:nt]
    v_refs = refs[nt:2 * nt]
    o_ref = refs[2 * nt]
    b = pl.program_id(0)
    g = pl.program_id(1)
    scale = ATT_HEAD_DIM ** -0.5
    lane = lax.broadcasted_iota(jnp.int32, (SUBLANES, PAGE_SIZE), 1)
    bf = lambda a: a.astype(BF16).astype(F32)
    for hh in range(SAMPLE_GROUP_HEADS):
        h = g * SAMPLE_GROUP_HEADS + hh
        q = jnp.broadcast_to(q_ref[0, hh] * scale, (SUBLANES, ATT_HEAD_DIM)).astype(BF16)
        slope = sl_ref[hh]
        logits = []
        for j in range(MOBA_TOPK):
            blk = idx_ref[(b * MOBA_TOPK + j) * ATT_HEADS + h]
            for half in range(BLOCK_PAGES):
                kpos = blk * MOBA_BLOCK + half * PAGE_SIZE + lane
                kt = k_refs[hh * N_SEL_TILES + j * BLOCK_PAGES + half][0, 0]
                s = _dot(q, kt.astype(BF16))
                logits.append(s - slope * (past_len - kpos).astype(F32))
        s_own = jnp.sum(q.astype(F32) * bf(kn_ref[0, hh]), axis=1, keepdims=True)
        m = s_own
        for s in logits:
            m = jnp.maximum(m, jnp.max(s, axis=1, keepdims=True))
        p_own = jnp.exp(s_own - m)
        l = p_own
        acc = bf(p_own) * bf(vn_ref[0, hh])
        for t, s in enumerate(logits):
            p = jnp.exp(s - m)
            l = l + jnp.sum(p, axis=1, keepdims=True)
            acc = acc + _dot_nt(p.astype(BF16), v_refs[hh * N_SEL_TILES + t][0, 0].astype(BF16))
        o = (acc / l) * _silu(zc_ref[0, hh])
        o_ref[0, hh] = o[0:1, :]


SAMPLE_GROUP_HEADS = 4


def _attn_sample(cache_kt, cache_vt, page_off, pt_flat, idx_flat, q4, kn4, vn4, zc4, n_pages):
    bsz = q4.shape[0]
    gh = SAMPLE_GROUP_HEADS
    past_len = n_pages * PAGE_SIZE
    slopes = jnp.broadcast_to(_alibi_slopes()[:, None, None], (ATT_HEADS, 1, PAGE_SIZE))

    def tile_spec(hh, j, half):
        def imap(b, g, pt, idx):
            h = g * gh + hh
            blk = idx[(b * MOBA_TOPK + j) * ATT_HEADS + h]
            return (page_off + pt[b * n_pages + blk * BLOCK_PAGES + half], h, 0, 0)
        return pl.BlockSpec((1, 1, ATT_HEAD_DIM, PAGE_SIZE), imap)

    tiles = [tile_spec(hh, j, half) for hh in range(gh) for j in range(MOBA_TOPK) for half in range(BLOCK_PAGES)]
    head = lambda: pl.BlockSpec((1, gh, 1, ATT_HEAD_DIM), lambda b, g, pt, idx: (b, g, 0, 0))
    return pl.pallas_call(
        functools.partial(_attn_sample_kernel, past_len=past_len),
        out_shape=jax.ShapeDtypeStruct((bsz, ATT_HEADS, 1, ATT_HEAD_DIM), F32),
        grid_spec=pltpu.PrefetchScalarGridSpec(
            num_scalar_prefetch=2,
            grid=(bsz, ATT_HEADS // gh),
            in_specs=[head(), head(), head(), head(),
                      pl.BlockSpec((gh, 1, PAGE_SIZE), lambda b, g, pt, idx: (g, 0, 0))] + tiles + tiles,
            out_specs=head()),
        compiler_params=_params("arbitrary", "arbitrary"),
        name="attn_sample",
    )(pt_flat, idx_flat, q4, kn4, vn4, zc4, slopes,
      *([cache_kt] * len(tiles)), *([cache_vt] * len(tiles)))


def _merge_kernel(oa_ref, ob_ref, oc_ref, ga_ref, gb_ref, gc_ref, w_ref, o_ref):
    acc = _sigmoid(ga_ref[...]) * _dot(oa_ref[...], w_ref[0])
    acc = acc + _sigmoid(gb_ref[...]) * _dot(ob_ref[...], w_ref[1])
    acc = acc + _sigmoid(gc_ref[...]) * _dot(oc_ref[...], w_ref[2])
    o_ref[...] = acc.astype(o_ref.dtype)


def _merge(o_a, o_b, o_c, proj, w_branch, tm, tn):
    m = o_a.shape[0]
    nt = D_MODEL // tn
    gate_blk0 = GATE_COLBLK * D_MODEL // tn
    br = lambda: pl.BlockSpec((tm, W_BRANCH), lambda j, i: (i, 0))
    gate = lambda r: pl.BlockSpec((tm, tn), lambda j, i: (i, gate_blk0 + r * nt + j))
    return pl.pallas_call(
        _merge_kernel,
        out_shape=jax.ShapeDtypeStruct((m, D_MODEL), BF16),
        grid=(nt, m // tm),
        in_specs=[br(), br(), br(), gate(0), gate(1), gate(2),
                  pl.BlockSpec((N_BRANCH, W_BRANCH, tn), lambda j, i: (0, 0, j))],
        out_specs=pl.BlockSpec((tm, tn), lambda j, i: (i, j)),
        compiler_params=_params("parallel", "parallel"),
        name="merge",
    )(o_a, o_b, o_c, proj, proj, proj, w_branch)


def _outproj_kernel(m_ref, w_ref, x_ref, g_ref, gn_ref, o_ref, h_ref):
    out = _dot(m_ref[...], w_ref[...])
    ms = jnp.mean(out * out, axis=-1, keepdims=True)
    y = x_ref[...] + out * lax.rsqrt(ms + EPS) * g_ref[...]
    o_ref[...] = y
    ms_y = jnp.mean(y * y, axis=-1, keepdims=True)
    h_ref[...] = (y * lax.rsqrt(ms_y + EPS) * gn_ref[...]).astype(h_ref.dtype)


def _outproj(merged, w_out, x, g_post, g_next, tm):
    m = merged.shape[0]
    rows = lambda: pl.BlockSpec((tm, D_MODEL), lambda i: (i, 0))
    gain = lambda: pl.BlockSpec((1, D_MODEL), lambda i: (0, 0))
    return pl.pallas_call(
        _outproj_kernel,
        out_shape=(jax.ShapeDtypeStruct((m, D_MODEL), F32), jax.ShapeDtypeStruct((m, D_MODEL), BF16)),
        grid=(m // tm,),
        in_specs=[rows(), pl.BlockSpec((D_MODEL, D_MODEL), lambda i: (0, 0)), rows(), gain(), gain()],
        out_specs=(rows(), rows()),
        compiler_params=_params("parallel"),
        name="outproj",
    )(merged, w_out, x, g_post.reshape(1, -1), g_next.reshape(1, -1))


def _seg(proj, blk, rows=None):
    out = proj[:, blk * W_BRANCH:(blk + 1) * W_BRANCH]
    return out if rows is None else out[:rows]


def _prompt_layer(x, h, bsz, T, lw, g_next, pool):
    (_, w_in, w_s, b_s, ln_g, ln_b, conv_w, conv_b, dt_bias, a_log, d_skip, g_ssm, w_br, w_out, g_post) = lw
    proj = _inproj(h, *w_in, min(INPROJ_TM, bsz * T))
    o_a = _gmlp(proj, w_s, b_s, ln_g, ln_b, 512)
    L = min(SSM_CHUNK, T)
    nc = T // L
    prev8 = jnp.zeros((bsz, SUBLANES, CONV_DIM), F32)
    h0 = jnp.zeros((bsz, SSM_HEADS * SSM_HEAD_DIM, SSM_STATE), F32)
    o_b, h_fin = _ssd(proj, XD_COLBLK, proj, COL_ZB, prev8, h0, conv_w, conv_b, dt_bias, a_log, d_skip, g_ssm,
                      bsz, nc, L, T)
    o_c, k_t, v_t, kmean = _attn_prompt(proj, bsz, T, pool)
    merged = _merge(o_a, o_b, o_c, proj, w_br, 512, 1024)
    y, h_next = _outproj(merged, w_out, x, g_post, g_next, 256)
    k_new = k_t.reshape(bsz, ATT_HEADS, ATT_HEAD_DIM, T).transpose(0, 3, 1, 2)
    v_new = v_t.reshape(bsz, ATT_HEADS, ATT_HEAD_DIM, T).transpose(0, 3, 1, 2)
    conv_new = proj.reshape(bsz, T, N_PROJ)[:, T - (SSM_CONV - 1):, XD_COLBLK * XD_WIDTH:][:, :, :CONV_DIM]
    ssm_new = h_fin.reshape(bsz, SSM_HEADS, SSM_HEAD_DIM, SSM_STATE)
    return y, h_next, k_new, v_new, ssm_new, conv_new, kmean


INPROJ_TM = 1024
SAMPLE_ROWS = 16
SAMPLE_L = 128


def _sample_layer(x16, h, bsz, conv_prev, ssm_prev, cache_kt, cache_vt, kmean, page_off, pt_flat, n_pages, lw,
                  g_next):
    (_, w_in, w_s, b_s, ln_g, ln_b, conv_w, conv_b, dt_bias, a_log, d_skip, g_ssm, w_br, w_out, g_post) = lw
    proj = _inproj(h, *w_in, SAMPLE_ROWS)
    o_a, v_a = _gmlp_single(proj, w_s, b_s, ln_g, ln_b)
    xd = proj[:bsz, XD_COLBLK * XD_WIDTH:]
    xd_rows = jnp.zeros((bsz, SAMPLE_L, XD_WIDTH), F32).at[:, 0].set(xd).reshape(bsz * SAMPLE_L, XD_WIDTH)
    zb_rows = jnp.zeros((bsz, SAMPLE_L, W_BRANCH), F32).at[:, 0].set(_seg(proj, COL_ZB, bsz))
    zb_rows = zb_rows.reshape(bsz * SAMPLE_L, W_BRANCH)
    prev8 = jnp.concatenate([jnp.zeros((bsz, SUBLANES - (SSM_CONV - 1), CONV_DIM), F32), conv_prev], axis=1)
    h0 = ssm_prev.reshape(bsz, SSM_HEADS * SSM_HEAD_DIM, SSM_STATE)
    ob_rows, h_fin = _ssd(xd_rows, 0, zb_rows, 0, prev8, h0, conv_w, conv_b, dt_bias, a_log, d_skip, g_ssm,
                          bsz, 1, SAMPLE_L, 1)
    o_b = ob_rows.reshape(bsz, SAMPLE_L, W_BRANCH)[:, 0]
    o_b = jnp.pad(o_b, ((0, SAMPLE_ROWS - bsz), (0, 0)))
    conv_new = jnp.concatenate([conv_prev[:, 1:], xd[:, None, :CONV_DIM]], axis=1)
    ssm_new = h_fin.reshape(bsz, SSM_HEADS, SSM_HEAD_DIM, SSM_STATE)
    q = _seg(proj, COL_Q, bsz)
    k_new = _seg(proj, COL_K, bsz)
    v_new = _seg(proj, COL_VA, bsz)
    top = _gate_topk(q.reshape(bsz, ATT_HEADS, ATT_HEAD_DIM), kmean)
    idx_flat = top[:, :MOBA_TOPK].reshape(-1)
    heads4 = lambda a: a.reshape(bsz, ATT_HEADS, 1, ATT_HEAD_DIM)
    att = _attn_sample(cache_kt, cache_vt, page_off, pt_flat, idx_flat, heads4(q), heads4(k_new), heads4(v_new),
                       heads4(_seg(proj, COL_ZC, bsz)), n_pages)
    o_c = jnp.pad(att.reshape(bsz, W_BRANCH), ((0, SAMPLE_ROWS - bsz), (0, 0))).astype(BF16)
    merged = _merge(o_a, o_b, o_c, proj, w_br, SAMPLE_ROWS, 1024)
    y, h_next = _outproj(merged, w_out, x16, g_post, g_next, SAMPLE_ROWS)
    shape4 = (bsz, 1, ATT_HEADS, ATT_HEAD_DIM)
    return (y, h_next, k_new.reshape(shape4), v_new.reshape(shape4), ssm_new, conv_new,
            v_a[:bsz].reshape(bsz, 1, W_BRANCH))


def kernel(x_prompt, x_sample, cache_k, cache_v, state_ssm, state_conv, page_table, g_pre, w_in, w_spatial,
           b_spatial, ln_v_g, ln_v_b, conv_w, conv_b, dt_bias, a_log, d_skip, g_ssm, w_branch, w_out, g_post):
    depth = w_in.shape[0]
    bp, T, _ = x_prompt.shape
    bs = x_sample.shape[0]
    n_phys = cache_k.shape[1]
    n_pages = page_table.shape[1]
    pt_flat = page_table.reshape(-1).astype(jnp.int32)
    xp = x_prompt.reshape(bp * T, D_MODEL)
    xs = jnp.pad(x_sample.reshape(bs, D_MODEL), ((0, SAMPLE_ROWS - bs), (0, 0)))
    to_t = lambda c: c.transpose(0, 1, 3, 4, 2).reshape(depth * n_phys, ATT_HEADS, ATT_HEAD_DIM, PAGE_SIZE)
    cache_kt, cache_vt = to_t(cache_k), to_t(cache_v)
    w_in_t = w_in.transpose(0, 2, 1)
    outs = [[] for _ in range(9)]
    hp = _prenorm(xp, g_pre[0], 256)
    hs = _prenorm(xs, g_pre[0], SAMPLE_ROWS)
    for l in range(depth):
        lw = (g_pre[l], (w_in_t, l), w_spatial[l], b_spatial[l], ln_v_g[l], ln_v_b[l], conv_w[l],
              conv_b[l], dt_bias[l], a_log[l], d_skip[l], g_ssm[l], w_branch[l].astype(BF16),
              w_out[l].astype(BF16), g_post[l])
        g_next = g_pre[(l + 1) % depth]
        pool = (cache_kt, l * n_phys, pt_flat, bs, n_pages)
        xp, hp, kp, vp, sp, cp, kmean = _prompt_layer(xp, hp, bp, T, lw, g_next, pool)
        xs, hs, ks, vs, ss, cs, gv = _sample_layer(xs, hs, bs, state_conv[l], state_ssm[l], cache_kt, cache_vt,
                                                   kmean, l * n_phys, pt_flat, n_pages, lw, g_next)
        for lst, val in zip(outs, (kp, vp, sp, cp, ks, vs, ss, cs, gv)):
            lst.append(val)
    return (xp.reshape(bp, T, D_MODEL), xs[:bs].reshape(bs, 1, D_MODEL)) + tuple(jnp.stack(o) for o in outs)
```

```python
import functools

import jax
import jax.numpy as jnp
import numpy as np
from jax import lax
from jax.experimental import pallas as pl
from jax.experimental.pallas import tpu as pltpu

F32 = jnp.float32
BF16 = jnp.bfloat16

D_MODEL = 2048
W_BRANCH = 1024
N_BRANCH = 3
GMLP_GROUPS = 4
GMLP_CHUNK = 128
GMLP_GW = W_BRANCH // GMLP_GROUPS
SSM_HEAD_DIM = 64
SSM_HEADS = 16
SSM_GROUPS = 2
SSM_STATE = 128
SSM_CONV = 4
SSM_CHUNK = 256
CONV_DIM = W_BRANCH + 2 * SSM_GROUPS * SSM_STATE
ATT_HEAD_DIM = 64
ATT_HEADS = 16
MOBA_BLOCK = 256
MOBA_TOPK = 3
PAGE_SIZE = 128
EPS = 1e-6
IN_SIZES = [W_BRANCH, W_BRANCH, W_BRANCH, CONV_DIM, W_BRANCH, SSM_HEADS,
            W_BRANCH, W_BRANCH, W_BRANCH, W_BRANCH, N_BRANCH * D_MODEL]

LANES = 128
SUBLANES = 8
VMEM_LIMIT = 56 * 1024 * 1024

COL_U, COL_V, COL_ZA, COL_ZB, COL_Q, COL_K, COL_VA, COL_ZC = range(8)
GATE_COLBLK = 4
XD_COLBLK = 7
XD_WIDTH = D_MODEL
N_PROJ = 8 * W_BRANCH + N_BRANCH * D_MODEL + XD_WIDTH
NEG = -1e30


def _params(*sem):
    return pltpu.CompilerParams(dimension_semantics=sem, vmem_limit_bytes=VMEM_LIMIT)


def _sigmoid(x):
    return 1.0 / (1.0 + jnp.exp(-x))


def _silu(x):
    return x * _sigmoid(x)


def _gelu(x):
    return 0.5 * x * (1.0 + jnp.tanh(0.7978845608028654 * (x + 0.044715 * (x * x * x))))


def _softplus(x):
    return jnp.maximum(x, 0.0) + jnp.log1p(jnp.exp(-jnp.abs(x)))


def _dot(a, b):
    return jnp.dot(a, b, preferred_element_type=F32)


def _dot_nt(a, b):
    return lax.dot_general(a, b, (((1,), (1,)), ((), ())), preferred_element_type=F32)


def _split3(x):
    x1 = x.astype(BF16)
    r1 = x - x1.astype(F32)
    x2 = r1.astype(BF16)
    r2 = r1 - x2.astype(F32)
    return x1, x2, r2.astype(BF16)


def _prenorm_kernel(x_ref, g_ref, o_ref):
    x = x_ref[...]
    ms = jnp.mean(x * x, axis=-1, keepdims=True)
    o_ref[...] = (x * lax.rsqrt(ms + EPS) * g_ref[...]).astype(o_ref.dtype)


def _prenorm(x, g, tm):
    m, d = x.shape
    return pl.pallas_call(
        _prenorm_kernel,
        out_shape=jax.ShapeDtypeStruct((m, d), BF16),
        grid=(m // tm,),
        in_specs=[pl.BlockSpec((tm, d), lambda i: (i, 0)),
                  pl.BlockSpec((1, d), lambda i: (0, 0))],
        out_specs=pl.BlockSpec((tm, d), lambda i: (i, 0)),
        compiler_params=_params("parallel"),
        name="prenorm",
    )(x, g.reshape(1, d))


W_HALF = 512
W_ROW_ALIGN = 16
PROJ_TN = 2 * W_HALF
BLOCK_PAGES = MOBA_BLOCK // PAGE_SIZE


def _w_in_row_offsets():
    start = dict(zip("u v za xbc zb dt q k va zc gl".split(), [0] + [int(c) for c in np.cumsum(IN_SIZES)[:-1]]))
    offs = []
    for name in "u v za zb q k va zc".split():
        offs += [start[name] + t * W_HALF for t in range(W_BRANCH // W_HALF)]
    offs += [start["gl"] + t * W_HALF for t in range(N_BRANCH * D_MODEL // W_HALF)]
    offs += [start["xbc"] + t * W_HALF for t in range(CONV_DIM // W_HALF)]
    offs += [start["dt"]]
    assert len(offs) * W_HALF == N_PROJ and max(offs) + W_HALF <= sum(IN_SIZES)
    assert all(o % W_ROW_ALIGN == 0 for o in offs)
    return jnp.asarray([o // W_ROW_ALIGN for o in offs], jnp.int32)


def _block_mean_keys(page_refs, km_ref):
    for i in range(len(page_refs) // BLOCK_PAGES):
        s = page_refs[i * BLOCK_PAGES][0]
        for j in range(1, BLOCK_PAGES):
            s = s + page_refs[i * BLOCK_PAGES + j][0]
        km_ref[0, i] = jnp.sum(s, axis=-1) * (1.0 / MOBA_BLOCK)


def _inproj_kernel(offs_ref, x_ref, w0_ref, w1_ref, o_ref, wbf):
    @pl.when(pl.program_id(1) == 0)
    def _():
        wbf[0:W_HALF, :] = w0_ref[...].astype(BF16)
        wbf[W_HALF:, :] = w1_ref[...].astype(BF16)

    o_ref[...] = _dot_nt(x_ref[...], wbf[...])


def _inproj(h, w_t_all, layer, tm):
    m, k = h.shape
    half = lambda t: pl.BlockSpec((pl.Squeezed(), pl.Element(W_HALF), pl.Element(k)),
                                  lambda j, i, offs: (layer, offs[2 * j + t] * W_ROW_ALIGN, 0))
    return pl.pallas_call(
        _inproj_kernel,
        out_shape=jax.ShapeDtypeStruct((m, N_PROJ), F32),
        grid_spec=pltpu.PrefetchScalarGridSpec(
            num_scalar_prefetch=1,
            grid=(N_PROJ // PROJ_TN, m // tm),
            in_specs=[pl.BlockSpec((tm, k), lambda j, i, offs: (i, 0)), half(0), half(1)],
            out_specs=pl.BlockSpec((tm, PROJ_TN), lambda j, i, offs: (i, j)),
            scratch_shapes=[pltpu.VMEM((PROJ_TN, k), BF16)]),
        compiler_params=_params("arbitrary", "arbitrary"),
        name="inproj",
    )(_w_in_row_offsets(), h, w_t_all, w_t_all)


def _layernorm_rows(x, g, b):
    mu = jnp.mean(x, axis=-1, keepdims=True)
    d = x - mu
    var = jnp.mean(d * d, axis=-1, keepdims=True)
    return d * lax.rsqrt(var + EPS) * g + b


def _gmlp_kernel(u_ref, v_ref, z_ref, ws_ref, bt_ref, lg_ref, lb_ref, o_ref, *, cc, nch):
    row = lax.broadcasted_iota(jnp.int32, (cc, cc), 0)
    col = lax.broadcasted_iota(jnp.int32, (cc, cc), 1)
    tri = row >= col
    for c in range(nch):
        sl = pl.ds(c * cc, cc)
        va = _layernorm_rows(_gelu(v_ref[sl, :]), lg_ref[...], lb_ref[...])
        for g in range(GMLP_GROUPS):
            gs = slice(g * GMLP_GW, (g + 1) * GMLP_GW)
            wc = jnp.where(tri, ws_ref[g], 0.0).astype(BF16)
            mixed = _dot(wc, va[:, gs].astype(BF16)) + bt_ref[:, g:g + 1]
            o = _gelu(u_ref[sl, gs]) * mixed * _silu(z_ref[sl, gs])
            o_ref[sl, gs] = o.astype(o_ref.dtype)


def _gmlp(proj, w_s, b_s, ln_g, ln_b, tr):
    m = proj.shape[0]
    cc = GMLP_CHUNK
    return pl.pallas_call(
        functools.partial(_gmlp_kernel, cc=cc, nch=tr // cc),
        out_shape=jax.ShapeDtypeStruct((m, W_BRANCH), BF16),
        grid=(m // tr,),
        in_specs=[pl.BlockSpec((tr, W_BRANCH), lambda i: (i, COL_U)),
                  pl.BlockSpec((tr, W_BRANCH), lambda i: (i, COL_V)),
                  pl.BlockSpec((tr, W_BRANCH), lambda i: (i, COL_ZA)),
                  pl.BlockSpec((GMLP_GROUPS, cc, cc), lambda i: (0, 0, 0)),
                  pl.BlockSpec((cc, GMLP_GROUPS), lambda i: (0, 0)),
                  pl.BlockSpec((1, W_BRANCH), lambda i: (0, 0)),
                  pl.BlockSpec((1, W_BRANCH), lambda i: (0, 0))],
        out_specs=pl.BlockSpec((tr, W_BRANCH), lambda i: (i, 0)),
        compiler_params=_params("parallel"),
        name="gmlp",
    )(proj, proj, proj, w_s, b_s.T, ln_g.reshape(1, -1), ln_b.reshape(1, -1))


def _gmlp1_kernel(u_ref, v_ref, z_ref, w0_ref, b0_ref, lg_ref, lb_ref, o_ref, va_ref):
    va = _layernorm_rows(_gelu(v_ref[...]), lg_ref[...], lb_ref[...])
    va_ref[...] = va
    mixed = w0_ref[...] * va + b0_ref[...]
    o_ref[...] = (_gelu(u_ref[...]) * mixed * _silu(z_ref[...])).astype(o_ref.dtype)


def _gmlp_single(proj, w_s, b_s, ln_g, ln_b):
    m = proj.shape[0]
    w0 = jnp.repeat(w_s[:, 0, 0], GMLP_GW).reshape(1, W_BRANCH)
    b0 = jnp.repeat(b_s[:, 0], GMLP_GW).reshape(1, W_BRANCH)
    row = lambda: pl.BlockSpec((1, W_BRANCH), lambda i: (0, 0))
    return pl.pallas_call(
        _gmlp1_kernel,
        out_shape=(jax.ShapeDtypeStruct((m, W_BRANCH), BF16),
                   jax.ShapeDtypeStruct((m, W_BRANCH), F32)),
        grid=(1,),
        in_specs=[pl.BlockSpec((m, W_BRANCH), lambda i: (0, COL_U)),
                  pl.BlockSpec((m, W_BRANCH), lambda i: (0, COL_V)),
                  pl.BlockSpec((m, W_BRANCH), lambda i: (0, COL_ZA)),
                  row(), row(), row(), row()],
        out_specs=(pl.BlockSpec((m, W_BRANCH), lambda i: (0, 0)),
                   pl.BlockSpec((m, W_BRANCH), lambda i: (0, 0))),
        compiler_params=_params("arbitrary"),
        name="gmlp_single",
    )(proj, proj, proj, w0, b0, ln_g.reshape(1, -1), ln_b.reshape(1, -1))


def _ssd_kernel(xd_ref, zb_ref, prev_ref, h0_ref, cw_ref, cb_ref, dtb_ref, alog_ref, dsk_ref, gs_ref,
                ob_ref, hfin_ref, state_scr, tail_scr, y_scr, xs_scr, *, L, t_valid, nc):
    c = pl.program_id(1)

    @pl.when(c == 0)
    def _():
        state_scr[...] = h0_ref[0]
        tail_scr[...] = prev_ref[0]

    raw = xd_ref[:, :CONV_DIM]
    tail = tail_scr[...]
    rid = lax.broadcasted_iota(jnp.int32, (SUBLANES, CONV_DIM), 0)
    acc = cb_ref[...] + raw * cw_ref[SSM_CONV - 1:SSM_CONV, :]
    for k in range(1, SSM_CONV):
        rolled = pltpu.roll(raw, k, axis=0)
        head = jnp.where(rid < k, pltpu.roll(tail, k, axis=0), rolled[:SUBLANES])
        shifted = jnp.concatenate([head, rolled[SUBLANES:]], axis=0)
        acc = acc + shifted * cw_ref[SSM_CONV - 1 - k:SSM_CONV - k, :]
    tail_scr[...] = raw[L - SUBLANES:, :]
    xbc = _silu(acc)
    x = xbc[:, :W_BRANCH]
    gn = SSM_GROUPS * SSM_STATE
    bm = xbc[:, W_BRANCH:W_BRANCH + gn].astype(BF16)
    cm = xbc[:, W_BRANCH + gn:].astype(BF16)

    dt = _softplus(xd_ref[:, CONV_DIM:CONV_DIM + LANES] + dtb_ref[...])
    if t_valid < nc * L:
        grow = c * L + lax.broadcasted_iota(jnp.int32, (L, LANES), 0)
        dt = jnp.where(grow < t_valid, dt, 0.0)
    da = dt * (-jnp.exp(alog_ref[...]))
    row = lax.broadcasted_iota(jnp.int32, (L, L), 0)
    col = lax.broadcasted_iota(jnp.int32, (L, L), 1)
    causal = row >= col
    tril = jnp.where(causal, 1.0, 0.0).astype(BF16)
    d1, d2, d3 = _split3(da)
    a_cs = _dot(tril, d1) + _dot(tril, d2) + _dot(tril, d3)
    a_cs_t = a_cs.T
    dt_t = dt.T
    hg = SSM_HEADS // SSM_GROUPS
    gw = hg * SSM_HEAD_DIM
    for g in range(SSM_GROUPS):
        bg = bm[:, g * SSM_STATE:(g + 1) * SSM_STATE]
        cg = cm[:, g * SSM_STATE:(g + 1) * SSM_STATE]
        cb = _dot_nt(cg, bg)
        st_g = state_scr[g * gw:(g + 1) * gw, :]
        y_off = _dot_nt(cg, st_g.astype(BF16))
        for hh in range(hg):
            h = g * hg + hh
            hs = slice(h * SSM_HEAD_DIM, (h + 1) * SSM_HEAD_DIM)
            ac = a_cs[:, h:h + 1]
            ar = a_cs_t[h:h + 1, :]
            seg = jnp.exp(jnp.where(causal, ac - ar, -jnp.inf))
            wm = (cb * seg * dt_t[h:h + 1, :]).astype(BF16)
            xh = x[:, hs]
            yd = _dot(wm, xh.astype(BF16))
            yo = y_off[:, hh * SSM_HEAD_DIM:(hh + 1) * SSM_HEAD_DIM] * jnp.exp(ac)
            y_scr[:, hs] = yd + yo
            a_last = a_cs[L - 1:L, h:h + 1]
            xs_scr[:, hs] = xh * (jnp.exp(a_last - ac) * dt[:, h:h + 1])
        sts = _dot(xs_scr[:, g * gw:(g + 1) * gw].T.astype(BF16), bg)
        for hh in range(hg):
            h = g * hg + hh
            rs = slice(h * SSM_HEAD_DIM, (h + 1) * SSM_HEAD_DIM)
            a_last = a_cs[L - 1:L, h:h + 1]
            state_scr[rs, :] = (jnp.exp(a_last) * state_scr[rs, :]
                                + sts[hh * SSM_HEAD_DIM:(hh + 1) * SSM_HEAD_DIM, :])

    gated = (y_scr[...] + dsk_ref[...] * x) * _silu(zb_ref[...])
    ms = jnp.mean(gated * gated, axis=-1, keepdims=True)
    ob_ref[...] = (gated * lax.rsqrt(ms + EPS) * gs_ref[...]).astype(ob_ref.dtype)

    @pl.when(c == nc - 1)
    def _():
        hfin_ref[0] = state_scr[...]


def _ssd(xd_arr, xd_colblk, zb_arr, zb_colblk, prev8, h0, conv_w, conv_b, dt_bias, a_log, d_skip, g_ssm,
         bsz, nc, L, t_valid):
    rows = bsz * nc * L
    pad16 = lambda a: jnp.pad(a, (0, LANES - SSM_HEADS)).reshape(1, LANES)
    const = lambda shape: pl.BlockSpec(shape, lambda b, c: (0,) * len(shape))
    return pl.pallas_call(
        functools.partial(_ssd_kernel, L=L, t_valid=t_valid, nc=nc),
        out_shape=(jax.ShapeDtypeStruct((rows, W_BRANCH), BF16),
                   jax.ShapeDtypeStruct((bsz, SSM_HEADS * SSM_HEAD_DIM, SSM_STATE), F32)),
        grid=(bsz, nc),
        in_specs=[pl.BlockSpec((L, XD_WIDTH), lambda b, c: (b * nc + c, xd_colblk)),
                  pl.BlockSpec((L, W_BRANCH), lambda b, c: (b * nc + c, zb_colblk)),
                  pl.BlockSpec((1, SUBLANES, CONV_DIM), lambda b, c: (b, 0, 0)),
                  pl.BlockSpec((1, SSM_HEADS * SSM_HEAD_DIM, SSM_STATE), lambda b, c: (b, 0, 0)),
                  const((SSM_CONV, CONV_DIM)), const((1, CONV_DIM)), const((1, LANES)), const((1, LANES)),
                  const((1, W_BRANCH)), const((1, W_BRANCH))],
        out_specs=(pl.BlockSpec((L, W_BRANCH), lambda b, c: (b * nc + c, 0)),
                   pl.BlockSpec((1, SSM_HEADS * SSM_HEAD_DIM, SSM_STATE), lambda b, c: (b, 0, 0))),
        scratch_shapes=[pltpu.VMEM((SSM_HEADS * SSM_HEAD_DIM, SSM_STATE), F32),
                        pltpu.VMEM((SUBLANES, CONV_DIM), F32),
                        pltpu.VMEM((L, W_BRANCH), F32),
                        pltpu.VMEM((L, W_BRANCH), F32)],
        compiler_params=_params("arbitrary", "arbitrary"),
        name="ssd",
    )(xd_arr, zb_arr, prev8, h0, conv_w, conv_b.reshape(1, -1), pad16(dt_bias), pad16(a_log),
      jnp.repeat(d_skip, SSM_HEAD_DIM).reshape(1, -1), g_ssm.reshape(1, -1))


def _select_topk(gate, n_elig, axis):
    idx = lax.broadcasted_iota(jnp.int32, gate.shape, axis).astype(F32)
    big = jnp.float32(1 << 30)
    g = jnp.where(idx < n_elig.astype(F32), gate, -jnp.inf)
    sel = jnp.zeros(gate.shape, jnp.bool_)
    for _ in range(MOBA_TOPK):
        m = jnp.max(g, axis=axis, keepdims=True)
        hit = (g == m) & (m > -jnp.inf) & (m < jnp.inf)
        first = jnp.min(jnp.where(hit, idx, big), axis=axis, keepdims=True)
        pick = idx == first
        sel = sel | pick
        g = jnp.where(pick, -jnp.inf, g)
    return sel


def _attn_kernel(pt_ref, q_ref, k_ref, v_ref, zc_ref, sl_ref, *refs, T, tq):
    n_pg = len(refs) - 7
    page_refs = refs[:n_pg]
    o_ref, kt_ref, vt_ref, pmean_ref, kbf, vtb, km_scr = refs[n_pg:]
    nb = T // MOBA_BLOCK
    nbp = km_scr.shape[0]
    qb = pl.program_id(2)

    @pl.when(qb == 0)
    def _():
        km_scr[...] = jnp.zeros(km_scr.shape, F32)
        for n in range(nb):
            rs = slice(n * MOBA_BLOCK, (n + 1) * MOBA_BLOCK)
            k = k_ref[rs, :]
            vt = v_ref[rs, :].T
            kt_ref[0, :, rs] = k.T
            vt_ref[0, :, rs] = vt
            kbf[n] = k.astype(BF16)
            vtb[n] = vt.astype(BF16)
            km_scr[n:n + 1, :] = jnp.mean(k, axis=0, keepdims=True)

    q = q_ref[...]
    lw = q.shape[1]
    lane = lax.broadcasted_iota(jnp.int32, (tq, lw), 1)
    krow = lax.broadcasted_iota(jnp.int32, (MOBA_BLOCK, tq), 0)
    qcol = lax.broadcasted_iota(jnp.int32, (MOBA_BLOCK, tq), 1)
    rel = (krow - qcol).astype(F32)
    brow = lax.broadcasted_iota(jnp.int32, (nbp, tq), 0)
    scale = ATT_HEAD_DIM ** -0.5
    km = km_scr[...]
    klane = lax.broadcasted_iota(jnp.int32, km.shape, 1)
    q1, q2, _ = _split3(q)
    nh = lw // ATT_HEAD_DIM
    qss, selbs, slopes, alibis = [], [], [], []
    for i in range(nh):
        k1, k2, _ = _split3(jnp.where(klane // ATT_HEAD_DIM == i, km, 0.0))
        gate = _dot_nt(k1, q1) + _dot_nt(k1, q2) + _dot_nt(k2, q1)
        selbs.append(jnp.where(_select_topk(gate, qb, 0), 0.0, NEG))
        qh = jnp.where(lane // ATT_HEAD_DIM == i, q, 0.0)
        qss.append((qh * scale).astype(BF16))
        slopes.append(sl_ref[0, :, i * ATT_HEAD_DIM:i * ATT_HEAD_DIM + 1])
        alibis.append(slopes[i] * rel)
    qs_all = jnp.concatenate(qss, axis=0)
    _block_mean_keys(page_refs, pmean_ref)

    def logits(n):
        s_all = _dot_nt(kbf[n], qs_all)
        out = []
        for i in range(nh):
            rowterm = (jnp.sum(jnp.where(brow == n, selbs[i], 0.0), axis=0, keepdims=True)
                       + slopes[i] * ((n - qb) * MOBA_BLOCK).astype(F32))
            out.append(s_all[:, i * tq:(i + 1) * tq] + alibis[i] + rowterm)
        return out

    def update(i, blocks, carry):
        m, l, acc = carry
        m_new = m
        for _, s in blocks:
            m_new = jnp.maximum(m_new, jnp.max(s, axis=0, keepdims=True))
        alpha = jnp.exp(m - m_new)
        l = alpha * l
        acc = alpha * acc
        for n, s in blocks:
            p = jnp.exp(s - m_new)
            l = l + jnp.sum(p, axis=0, keepdims=True)
            acc = acc + _dot(vtb[n, pl.ds(i * ATT_HEAD_DIM, ATT_HEAD_DIM), :], p.astype(BF16))
        return m_new, l, acc

    def body(j, carry):
        n0 = 2 * j
        sa, sb = logits(n0), logits(n0 + 1)
        return tuple(update(i, [(n0, sa[i]), (n0 + 1, sb[i])], carry[i]) for i in range(nh))

    init = (jnp.full((1, tq), NEG, F32), jnp.zeros((1, tq), F32), jnp.zeros((ATT_HEAD_DIM, tq), F32))
    carry = lax.fori_loop(0, qb // 2, body, (init,) * nh)

    def finish(with_left):
        left = qb - 1
        s_left = logits(left) if with_left else None
        s_qk = _dot_nt(kbf[qb], qs_all)
        outs = []
        for i in range(nh):
            s_own = jnp.where(krow <= qcol, s_qk[:, i * tq:(i + 1) * tq] + alibis[i], NEG)
            blocks = ([(left, s_left[i])] if with_left else []) + [(qb, s_own)]
            m, l, acc = update(i, blocks, carry[i])
            outs.append(acc / l)
        return jnp.concatenate(outs, axis=0)

    o_t = lax.cond(qb % 2 == 1, lambda: finish(True), lambda: finish(False))
    o_ref[...] = (o_t.T * _silu(zc_ref[...])).astype(o_ref.dtype)


ATT_GROUP_HEADS = 4


def _alibi_slopes():
    return 2.0 ** (-8.0 * jnp.arange(1, ATT_HEADS + 1, dtype=F32) / ATT_HEADS)


def _attn_prompt(proj, bsz, T, pool):
    tq = MOBA_BLOCK
    nq = T // tq
    nb = T // MOBA_BLOCK
    nbp = -(-nb // SUBLANES) * SUBLANES
    lw = ATT_GROUP_HEADS * ATT_HEAD_DIM
    hp_n = W_BRANCH // lw
    slopes = jnp.repeat(_alibi_slopes(), ATT_HEAD_DIM).reshape(hp_n, 1, lw)
    cpb = W_BRANCH // lw
    cache_t, page_off, pt_flat, n_seq, n_pages = pool
    steps = bsz * hp_n * nq
    pps = n_seq * n_pages // steps
    bps = pps // BLOCK_PAGES
    assert pps * steps == n_seq * n_pages and bps * BLOCK_PAGES == pps and n_pages % pps == 0
    spb = n_pages // pps
    step = lambda b, hp, qi: (b * hp_n + hp) * nq + qi

    def page_spec(t):
        return pl.BlockSpec((1, ATT_HEADS, ATT_HEAD_DIM, PAGE_SIZE),
                            lambda b, hp, qi, pt: (page_off + pt[step(b, hp, qi) * pps + t], 0, 0, 0))

    kv_t = lambda: pl.BlockSpec((1, lw, T), lambda b, hp, qi, pt: (b, hp, 0))
    return pl.pallas_call(
        functools.partial(_attn_kernel, T=T, tq=tq),
        out_shape=(jax.ShapeDtypeStruct((bsz * T, W_BRANCH), BF16),
                   jax.ShapeDtypeStruct((bsz, W_BRANCH, T), F32),
                   jax.ShapeDtypeStruct((bsz, W_BRANCH, T), F32),
                   jax.ShapeDtypeStruct((n_seq, n_pages // BLOCK_PAGES, ATT_HEADS, ATT_HEAD_DIM), F32)),
        grid_spec=pltpu.PrefetchScalarGridSpec(
            num_scalar_prefetch=1,
            grid=(bsz, hp_n, nq),
            in_specs=[pl.BlockSpec((tq, lw), lambda b, hp, qi, pt: (b * nq + qi, COL_Q * cpb + hp)),
                      pl.BlockSpec((T, lw), lambda b, hp, qi, pt: (b, COL_K * cpb + hp)),
                      pl.BlockSpec((T, lw), lambda b, hp, qi, pt: (b, COL_VA * cpb + hp)),
                      pl.BlockSpec((tq, lw), lambda b, hp, qi, pt: (b * nq + qi, COL_ZC * cpb + hp)),
                      pl.BlockSpec((1, 1, lw), lambda b, hp, qi, pt: (hp, 0, 0))]
                     + [page_spec(t) for t in range(pps)],
            out_specs=(pl.BlockSpec((tq, lw), lambda b, hp, qi, pt: (b * nq + qi, hp)), kv_t(), kv_t(),
                       pl.BlockSpec((1, bps, ATT_HEADS, ATT_HEAD_DIM),
                                    lambda b, hp, qi, pt: (step(b, hp, qi) // spb, step(b, hp, qi) % spb, 0, 0))),
            scratch_shapes=[pltpu.VMEM((nb, MOBA_BLOCK, lw), BF16), pltpu.VMEM((nb, lw, MOBA_BLOCK), BF16),
                            pltpu.VMEM((nbp, lw), F32)]),
        compiler_params=_params("arbitrary", "arbitrary", "arbitrary"),
        name="attn_prompt",
    )(pt_flat, proj, proj, proj, proj, slopes, *([cache_t] * pps))


def _gate_topk_kernel(q_ref, km_ref, o_ref):
    gate = jnp.sum(km_ref[0] * q_ref[...], axis=-1)
    idx = lax.broadcasted_iota(jnp.int32, gate.shape, 0).astype(F32)
    rid = lax.broadcasted_iota(jnp.int32, (SUBLANES, ATT_HEADS), 0)
    big = jnp.float32(1 << 30)
    out = jnp.zeros((SUBLANES, ATT_HEADS), F32)
    g = gate
    for r in range(MOBA_TOPK):
        m = jnp.max(g, axis=0, keepdims=True)
        first = jnp.min(jnp.where(g == m, idx, big), axis=0, keepdims=True)
        out = jnp.where(rid == r, jnp.broadcast_to(first, (SUBLANES, ATT_HEADS)), out)
        g = jnp.where(idx == first, -jnp.inf, g)
    o_ref[0] = out.astype(jnp.int32)


def _gate_topk(q3, kmean):
    bsz, nblk = kmean.shape[:2]
    return pl.pallas_call(
        _gate_topk_kernel,
        out_shape=jax.ShapeDtypeStruct((bsz, SUBLANES, ATT_HEADS), jnp.int32),
        grid=(bsz,),
        in_specs=[pl.BlockSpec((1, ATT_HEADS, ATT_HEAD_DIM), lambda b: (b, 0, 0)),
                  pl.BlockSpec((1, nblk, ATT_HEADS, ATT_HEAD_DIM), lambda b: (b, 0, 0, 0))],
        out_specs=pl.BlockSpec((1, SUBLANES, ATT_HEADS), lambda b: (b, 0, 0)),
        compiler_params=_params("arbitrary"),
        name="gate_topk",
    )(q3, kmean)


N_SEL_TILES = MOBA_TOPK * BLOCK_PAGES


def _attn_sample_kernel(pt_ref, idx_ref, q_ref, kn_ref, vn_ref, zc_ref, sl_ref, *refs, past_len):
    nt = SAMPLE_GROUP_HEADS * N_SEL_TILES
    k_refs = refs[:nt]
    v_refs = refs[nt:2 * nt]
    o_ref = refs[2 * nt]
    b = pl.program_id(0)
    g = pl.program_id(1)
    scale = ATT_HEAD_DIM ** -0.5
    lane = lax.broadcasted_iota(jnp.int32, (SUBLANES, PAGE_SIZE), 1)
    bf = lambda a: a.astype(BF16).astype(F32)
    for hh in range(SAMPLE_GROUP_HEADS):
        h = g * SAMPLE_GROUP_HEADS + hh
        q = jnp.broadcast_to(q_ref[0, hh] * scale, (SUBLANES, ATT_HEAD_DIM)).astype(BF16)
        slope = sl_ref[hh]
        logits = []
        for j in range(MOBA_TOPK):
            blk = idx_ref[(b * MOBA_TOPK + j) * ATT_HEADS + h]
            for half in range(BLOCK_PAGES):
                kpos = blk * MOBA_BLOCK + half * PAGE_SIZE + lane
                kt = k_refs[hh * N_SEL_TILES + j * BLOCK_PAGES + half][0, 0]
                s = _dot(q, kt.astype(BF16))
                logits.append(s - slope * (past_len - kpos).astype(F32))
        s_own = jnp.sum(q.astype(F32) * bf(kn_ref[0, hh]), axis=1, keepdims=True)
        m = s_own
        for s in logits:
            m = jnp.maximum(m, jnp.max(s, axis=1, keepdims=True))
        p_own = jnp.exp(s_own - m)
        l = p_own
        acc = bf(p_own) * bf(vn_ref[0, hh])
        for t, s in enumerate(logits):
            p = jnp.exp(s - m)
            l = l + jnp.sum(p, axis=1, keepdims=True)
            acc = acc + _dot_nt(p.astype(BF16), v_refs[hh * N_SEL_TILES + t][0, 0].astype(BF16))
        o = (acc / l) * _silu(zc_ref[0, hh])
        o_ref[0, hh] = o[0:1, :]


SAMPLE_GROUP_HEADS = 4


def _attn_sample(cache_kt, cache_vt, page_off, pt_flat, idx_flat, q4, kn4, vn4, zc4, n_pages):
    bsz = q4.shape[0]
    gh = SAMPLE_GROUP_HEADS
    past_len = n_pages * PAGE_SIZE
    slopes = jnp.broadcast_to(_alibi_slopes()[:, None, None], (ATT_HEADS, 1, PAGE_SIZE))

    def tile_spec(hh, j, half):
        def imap(b, g, pt, idx):
            h = g * gh + hh
            blk = idx[(b * MOBA_TOPK + j) * ATT_HEADS + h]
            return (page_off + pt[b * n_pages + blk * BLOCK_PAGES + half], h, 0, 0)
        return pl.BlockSpec((1, 1, ATT_HEAD_DIM, PAGE_SIZE), imap)

    tiles = [tile_spec(hh, j, half) for hh in range(gh) for j in range(MOBA_TOPK) for half in range(BLOCK_PAGES)]
    head = lambda: pl.BlockSpec((1, gh, 1, ATT_HEAD_DIM), lambda b, g, pt, idx: (b, g, 0, 0))
    return pl.pallas_call(
        functools.partial(_attn_sample_kernel, past_len=past_len),
        out_shape=jax.ShapeDtypeStruct((bsz, ATT_HEADS, 1, ATT_HEAD_DIM), F32),
        grid_spec=pltpu.PrefetchScalarGridSpec(
            num_scalar_prefetch=2,
            grid=(bsz, ATT_HEADS // gh),
            in_specs=[head(), head(), head(), head(),
                      pl.BlockSpec((gh, 1, PAGE_SIZE), lambda b, g, pt, idx: (g, 0, 0))] + tiles + tiles,
            out_specs=head()),
        compiler_params=_params("arbitrary", "arbitrary"),
        name="attn_sample",
    )(pt_flat, idx_flat, q4, kn4, vn4, zc4, slopes,
      *([cache_kt] * len(tiles)), *([cache_vt] * len(tiles)))


def _merge_kernel(oa_ref, ob_ref, oc_ref, ga_ref, gb_ref, gc_ref, w_ref, o_ref):
    acc = _sigmoid(ga_ref[...]) * _dot(oa_ref[...], w_ref[0])
    acc = acc + _sigmoid(gb_ref[...]) * _dot(ob_ref[...], w_ref[1])
    acc = acc + _sigmoid(gc_ref[...]) * _dot(oc_ref[...], w_ref[2])
    o_ref[...] = acc.astype(o_ref.dtype)


def _merge(o_a, o_b, o_c, proj, w_branch, tm, tn):
    m = o_a.shape[0]
    nt = D_MODEL // tn
    gate_blk0 = GATE_COLBLK * D_MODEL // tn
    br = lambda: pl.BlockSpec((tm, W_BRANCH), lambda j, i: (i, 0))
    gate = lambda r: pl.BlockSpec((tm, tn), lambda j, i: (i, gate_blk0 + r * nt + j))
    return pl.pallas_call(
        _merge_kernel,
        out_shape=jax.ShapeDtypeStruct((m, D_MODEL), BF16),
        grid=(nt, m // tm),
        in_specs=[br(), br(), br(), gate(0), gate(1), gate(2),
                  pl.BlockSpec((N_BRANCH, W_BRANCH, tn), lambda j, i: (0, 0, j))],
        out_specs=pl.BlockSpec((tm, tn), lambda j, i: (i, j)),
        compiler_params=_params("parallel", "parallel"),
        name="merge",
    )(o_a, o_b, o_c, proj, proj, proj, w_branch)


def _outproj_kernel(m_ref, w_ref, x_ref, g_ref, gn_ref, o_ref, h_ref):
    out = _dot(m_ref[...], w_ref[...])
    ms = jnp.mean(out * out, axis=-1, keepdims=True)
    y = x_ref[...] + out * lax.rsqrt(ms + EPS) * g_ref[...]
    o_ref[...] = y
    ms_y = jnp.mean(y * y, axis=-1, keepdims=True)
    h_ref[...] = (y * lax.rsqrt(ms_y + EPS) * gn_ref[...]).astype(h_ref.dtype)


def _outproj(merged, w_out, x, g_post, g_next, tm):
    m = merged.shape[0]
    rows = lambda: pl.BlockSpec((tm, D_MODEL), lambda i: (i, 0))
    gain = lambda: pl.BlockSpec((1, D_MODEL), lambda i: (0, 0))
    return pl.pallas_call(
        _outproj_kernel,
        out_shape=(jax.ShapeDtypeStruct((m, D_MODEL), F32), jax.ShapeDtypeStruct((m, D_MODEL), BF16)),
        grid=(m // tm,),
        in_specs=[rows(), pl.BlockSpec((D_MODEL, D_MODEL), lambda i: (0, 0)), rows(), gain(), gain()],
        out_specs=(rows(), rows()),
        compiler_params=_params("parallel"),
        name="outproj",
    )(merged, w_out, x, g_post.reshape(1, -1), g_next.reshape(1, -1))


def _seg(proj, blk, rows=None):
    out = proj[:, blk * W_BRANCH:(blk + 1) * W_BRANCH]
    return out if rows is None else out[:rows]


def _prompt_layer(x, h, bsz, T, lw, g_next, pool):
    (_, w_in, w_s, b_s, ln_g, ln_b, conv_w, conv_b, dt_bias, a_log, d_skip, g_ssm, w_br, w_out, g_post) = lw
    proj = _inproj(h, *w_in, min(INPROJ_TM, bsz * T))
    o_a = _gmlp(proj, w_s, b_s, ln_g, ln_b, 512)
    L = min(SSM_CHUNK, T)
    nc = T // L
    prev8 = jnp.zeros((bsz, SUBLANES, CONV_DIM), F32)
    h0 = jnp.zeros((bsz, SSM_HEADS * SSM_HEAD_DIM, SSM_STATE), F32)
    o_b, h_fin = _ssd(proj, XD_COLBLK, proj, COL_ZB, prev8, h0, conv_w, conv_b, dt_bias, a_log, d_skip, g_ssm,
                      bsz, nc, L, T)
    o_c, k_t, v_t, kmean = _attn_prompt(proj, bsz, T, pool)
    merged = _merge(o_a, o_b, o_c, proj, w_br, 512, 1024)
    y, h_next = _outproj(merged, w_out, x, g_post, g_next, 256)
    k_new = k_t.reshape(bsz, ATT_HEADS, ATT_HEAD_DIM, T).transpose(0, 3, 1, 2)
    v_new = v_t.reshape(bsz, ATT_HEADS, ATT_HEAD_DIM, T).transpose(0, 3, 1, 2)
    conv_new = proj.reshape(bsz, T, N_PROJ)[:, T - (SSM_CONV - 1):, XD_COLBLK * XD_WIDTH:][:, :, :CONV_DIM]
    ssm_new = h_fin.reshape(bsz, SSM_HEADS, SSM_HEAD_DIM, SSM_STATE)
    return y, h_next, k_new, v_new, ssm_new, conv_new, kmean


INPROJ_TM = 1024
SAMPLE_ROWS = 16
SAMPLE_L = 128


def _sample_layer(x16, h, bsz, conv_prev, ssm_prev, cache_kt, cache_vt, kmean, page_off, pt_flat, n_pages, lw,
                  g_next):
    (_, w_in, w_s, b_s, ln_g, ln_b, conv_w, conv_b, dt_bias, a_log, d_skip, g_ssm, w_br, w_out, g_post) = lw
    proj = _inproj(h, *w_in, SAMPLE_ROWS)
    o_a, v_a = _gmlp_single(proj, w_s, b_s, ln_g, ln_b)
    xd = proj[:bsz, XD_COLBLK * XD_WIDTH:]
    xd_rows = jnp.zeros((bsz, SAMPLE_L, XD_WIDTH), F32).at[:, 0].set(xd).reshape(bsz * SAMPLE_L, XD_WIDTH)
    zb_rows = jnp.zeros((bsz, SAMPLE_L, W_BRANCH), F32).at[:, 0].set(_seg(proj, COL_ZB, bsz))
    zb_rows = zb_rows.reshape(bsz * SAMPLE_L, W_BRANCH)
    prev8 = jnp.concatenate([jnp.zeros((bsz, SUBLANES - (SSM_CONV - 1), CONV_DIM), F32), conv_prev], axis=1)
    h0 = ssm_prev.reshape(bsz, SSM_HEADS * SSM_HEAD_DIM, SSM_STATE)
    ob_rows, h_fin = _ssd(xd_rows, 0, zb_rows, 0, prev8, h0, conv_w, conv_b, dt_bias, a_log, d_skip, g_ssm,
                          bsz, 1, SAMPLE_L, 1)
    o_b = ob_rows.reshape(bsz, SAMPLE_L, W_BRANCH)[:, 0]
    o_b = jnp.pad(o_b, ((0, SAMPLE_ROWS - bsz), (0, 0)))
    conv_new = jnp.concatenate([conv_prev[:, 1:], xd[:, None, :CONV_DIM]], axis=1)
    ssm_new = h_fin.reshape(bsz, SSM_HEADS, SSM_HEAD_DIM, SSM_STATE)
    q = _seg(proj, COL_Q, bsz)
    k_new = _seg(proj, COL_K, bsz)
    v_new = _seg(proj, COL_VA, bsz)
    top = _gate_topk(q.reshape(bsz, ATT_HEADS, ATT_HEAD_DIM), kmean)
    idx_flat = top[:, :MOBA_TOPK].reshape(-1)
    heads4 = lambda a: a.reshape(bsz, ATT_HEADS, 1, ATT_HEAD_DIM)
    att = _attn_sample(cache_kt, cache_vt, page_off, pt_flat, idx_flat, heads4(q), heads4(k_new), heads4(v_new),
                       heads4(_seg(proj, COL_ZC, bsz)), n_pages)
    o_c = jnp.pad(att.reshape(bsz, W_BRANCH), ((0, SAMPLE_ROWS - bsz), (0, 0))).astype(BF16)
    merged = _merge(o_a, o_b, o_c, proj, w_br, SAMPLE_ROWS, 1024)
    y, h_next = _outproj(merged, w_out, x16, g_post, g_next, SAMPLE_ROWS)
    shape4 = (bsz, 1, ATT_HEADS, ATT_HEAD_DIM)
    return (y, h_next, k_new.reshape(shape4), v_new.reshape(shape4), ssm_new, conv_new,
            v_a[:bsz].reshape(bsz, 1, W_BRANCH))


def kernel(x_prompt, x_sample, cache_k, cache_v, state_ssm, state_conv, page_table, g_pre, w_in, w_spatial,
           b_spatial, ln_v_g, ln_v_b, conv_w, conv_b, dt_bias, a_log, d_skip, g_ssm, w_branch, w_out, g_post):
    depth = w_in.shape[0]
    bp, T, _ = x_prompt.shape
    bs = x_sample.shape[0]
    n_phys = cache_k.shape[1]
    n_pages = page_table.shape[1]
    pt_flat = page_table.reshape(-1).astype(jnp.int32)
    xp = x_prompt.reshape(bp * T, D_MODEL)
    xs = jnp.pad(x_sample.reshape(bs, D_MODEL), ((0, SAMPLE_ROWS - bs), (0, 0)))
    to_t = lambda c: c.transpose(0, 1, 3, 4, 2).reshape(depth * n_phys, ATT_HEADS, ATT_HEAD_DIM, PAGE_SIZE)
    cache_kt, cache_vt = to_t(cache_k), to_t(cache_v)
    w_in_t = w_in.transpose(0, 2, 1)
    outs = [[] for _ in range(9)]
    hp = _prenorm(xp, g_pre[0], 256)
    hs = _prenorm(xs, g_pre[0], SAMPLE_ROWS)
    for l in range(depth):
        lw = (g_pre[l], (w_in_t, l), w_spatial[l], b_spatial[l], ln_v_g[l], ln_v_b[l], conv_w[l],
              conv_b[l], dt_bias[l], a_log[l], d_skip[l], g_ssm[l], w_branch[l].astype(BF16),
              w_out[l].astype(BF16), g_post[l])
        g_next = g_pre[(l + 1) % depth]
        pool = (cache_kt, l * n_phys, pt_flat, bs, n_pages)
        xp, hp, kp, vp, sp, cp, kmean = _prompt_layer(xp, hp, bp, T, lw, g_next, pool)
        xs, hs, ks, vs, ss, cs, gv = _sample_layer(xs, hs, bs, state_conv[l], state_ssm[l], cache_kt, cache_vt,
                                                   kmean, l * n_phys, pt_flat, n_pages, lw, g_next)
        for lst, val in zip(outs, (kp, vp, sp, cp, ks, vs, ss, cs, gv)):
            lst.append(val)
    return (xp.reshape(bp, T, D_MODEL), xs[:bs].reshape(bs, 1, D_MODEL)) + tuple(jnp.stack(o) for o in outs)
```
